```python
import math
import jax, jax.numpy as jnp
from jax import lax
import numpy as np

D_MODEL = 2048
BATCH = 2
SEQ = 16384
DEPTH = 1
DEC_BATCH = 2
DEC_SEQ = 4096
PAST_LEN = 128

N_META = 16
BLOCK = 128
WINDOW = 128
A_HEADS = 16
A_KV_HEADS = 4
A_HEAD_DIM = 128
A_GROUP = A_HEADS // A_KV_HEADS
B_HEADS = 16
Q_LORA = 512
KV_LORA = 256
QK_NOPE = 128
QK_ROPE = 64
V_HEAD = 128
ROPE_THETA = 10000.0
REL_BUCKETS = 32
REL_MAX_DIST = 128
PEER_HEADS = 8
N_KEYS = 128
N_EXPERTS = N_KEYS * N_KEYS
PEER_KEY_DIM = 128
PEER_TOPK = 16
PEER_CHUNK = 256
EPS = 1e-6
NEG = -1e30

A_Q_COLS = A_HEADS * A_HEAD_DIM
A_KV_COLS = A_KV_HEADS * A_HEAD_DIM
B_OUT_COLS = B_HEADS * V_HEAD
GATE_COLS = 2 * D_MODEL
IN_SIZES = (A_Q_COLS, A_KV_COLS, A_KV_COLS, Q_LORA, KV_LORA, QK_ROPE, GATE_COLS)
IN_SPLITS = (A_Q_COLS,
             A_Q_COLS + A_KV_COLS,
             A_Q_COLS + 2 * A_KV_COLS,
             A_Q_COLS + 2 * A_KV_COLS + Q_LORA,
             A_Q_COLS + 2 * A_KV_COLS + Q_LORA + KV_LORA,
             A_Q_COLS + 2 * A_KV_COLS + Q_LORA + KV_LORA + QK_ROPE)
D_IN = A_Q_COLS + 2 * A_KV_COLS + Q_LORA + KV_LORA + QK_ROPE + GATE_COLS

kernel_name = "hybrid_swa_mla_peer_encoder"


def rmsnorm(x, g):
    xf = x.astype(jnp.float32)
    y = xf * lax.rsqrt(jnp.mean(xf * xf, axis=-1, keepdims=True) + EPS)
    return (y * g.astype(jnp.float32)).astype(x.dtype)


def rel_bucket(rel):
    nb = REL_BUCKETS // 2
    max_exact = nb // 2
    ret = jnp.where(rel > 0, nb, 0)
    n = jnp.abs(rel)
    nf = jnp.maximum(n, 1).astype(jnp.float32)
    large = max_exact + (jnp.log(nf / max_exact) / math.log(REL_MAX_DIST / max_exact)
                         * (nb - max_exact)).astype(jnp.int32)
    large = jnp.minimum(large, nb - 1)
    return ret + jnp.where(n < max_exact, n, large)


def sink_softmax(parts, sink):
    m = sink
    for s in parts:
        m = jnp.maximum(m, jnp.max(s, axis=-1, keepdims=True))
    ps = [jnp.exp(s - m) for s in parts]
    denom = jnp.exp(sink - m)
    for p in ps:
        denom = denom + jnp.sum(p, axis=-1, keepdims=True)
    return [p / denom for p in ps]


def rope(x, pos):
    half = QK_ROPE // 2
    freqs = ROPE_THETA ** (-jnp.arange(half, dtype=jnp.float32) / half)
    ang = pos.astype(jnp.float32)[:, None] * freqs[None, :]
    c = jnp.cos(ang)[:, None, :].astype(x.dtype)
    s = jnp.sin(ang)[:, None, :].astype(x.dtype)
    x1, x2 = x[..., :half], x[..., half:]
    return jnp.concatenate([x1 * c - x2 * s, x2 * c + x1 * s], axis=-1)


def window_sink_attention(q, k, v, sink, rel_table, keep_meta):
    bsz, L = q.shape[0], q.shape[1]
    S = L - N_META
    n = S // BLOCK
    scale = A_HEAD_DIM ** -0.5
    sink_b = sink.astype(jnp.float32).reshape(A_KV_HEADS, A_GROUP, 1, 1)
    km, vm = k[:, :N_META], v[:, :N_META]

    def banded(t):
        tp = jnp.pad(t, ((0, 0), (BLOCK, BLOCK), (0, 0), (0, 0)))
        tb = tp.reshape(bsz, n + 2, BLOCK, A_KV_HEADS, A_HEAD_DIM)
        return jnp.concatenate([tb[:, :-2], tb[:, 1:-1], tb[:, 2:]], axis=2)

    kb, vb = banded(k[:, N_META:]), banded(v[:, N_META:])
    qb = q[:, N_META:].reshape(bsz, n, BLOCK, A_KV_HEADS, A_GROUP, A_HEAD_DIM)

    i = jnp.arange(BLOCK)
    c = jnp.arange(3 * BLOCK)
    blk = jnp.arange(n)
    rel_w = c[None, :] - BLOCK - i[:, None]
    key_idx = (blk[:, None] - 1) * BLOCK + c[None, :]
    valid_w = (jnp.abs(rel_w) <= WINDOW)[None] & ((key_idx >= 0) & (key_idx < S))[:, None, :]
    bias_w = rel_table[rel_bucket(rel_w)].astype(jnp.float32)
    bias_w = bias_w.transpose(2, 0, 1).reshape(A_KV_HEADS, A_GROUP, BLOCK, 3 * BLOCK)
    q_pos = N_META + blk[:, None] * BLOCK + i[None, :]
    rel_m = jnp.arange(N_META)[None, None, :] - q_pos[:, :, None]
    bias_m = rel_table[rel_bucket(rel_m)].astype(jnp.float32)
    bias_m = bias_m.transpose(0, 3, 1, 2).reshape(n, A_KV_HEADS, A_GROUP, BLOCK, N_META)

    s_w = jnp.einsum('bnqkgd,bnckd->bnkgqc', qb, kb).astype(jnp.float32) * scale + bias_w
    s_w = jnp.where(valid_w[:, None, None], s_w, NEG)
    s_m = jnp.einsum('bnqkgd,bmkd->bnkgqm', qb, km).astype(jnp.float32) * scale + bias_m
    p_w, p_m = sink_softmax([s_w, s_m], sink_b)
    o = (jnp.einsum('bnkgqc,bnckd->bnqkgd', p_w.astype(v.dtype), vb)
         + jnp.einsum('bnkgqm,bmkd->bnqkgd', p_m.astype(v.dtype), vm))
    o = o.reshape(bsz, S, A_HEADS * A_HEAD_DIM)

    if keep_meta:
        qm = q[:, :N_META].reshape(bsz, N_META, A_KV_HEADS, A_GROUP, A_HEAD_DIM)
        kk, vv = k[:, :N_META + BLOCK], v[:, :N_META + BLOCK]
        rel = jnp.arange(N_META + BLOCK)[None, :] - jnp.arange(N_META)[:, None]
        bias = rel_table[rel_bucket(rel)].astype(jnp.float32).transpose(2, 0, 1)
        bias = bias.reshape(A_KV_HEADS, A_GROUP, N_META, N_META + BLOCK)
        s = jnp.einsum('bqkgd,bckd->bkgqc', qm, kk).astype(jnp.float32) * scale + bias
        s = jnp.where(jnp.abs(rel) <= WINDOW, s, NEG)
        (p,) = sink_softmax([s], sink_b)
        om = jnp.einsum('bkgqc,bckd->bqkgd', p.astype(v.dtype), vv).reshape(bsz, N_META, A_HEADS * A_HEAD_DIM)
        o = jnp.concatenate([om, o], axis=1)
    return o


def mla_attention(q_nope, q_rope, k_nope, k_rope, v, keep_meta):
    bsz, L = q_nope.shape[0], q_nope.shape[1]
    S = L - N_META
    n = S // BLOCK
    scale = (QK_NOPE + QK_ROPE) ** -0.5

    def attend(qn, qr):
        s = (jnp.einsum('bqhd,bkhd->bhqk', qn, k_nope)
             + jnp.einsum('bqhd,bkd->bhqk', qr, k_rope)).astype(jnp.float32) * scale
        p = jax.nn.softmax(s, axis=-1)
        return jnp.einsum('bhqk,bkhd->bqhd', p.astype(v.dtype), v)

    qn_b = jnp.moveaxis(q_nope[:, N_META:].reshape(bsz, n, BLOCK, B_HEADS, QK_NOPE), 1, 0)
    qr_b = jnp.moveaxis(q_rope[:, N_META:].reshape(bsz, n, BLOCK, B_HEADS, QK_ROPE), 1, 0)
    o = lax.map(lambda a: attend(a[0], a[1]), (qn_b, qr_b))
    o = jnp.moveaxis(o, 0, 1).reshape(bsz, S, B_HEADS * V_HEAD)
    if keep_meta:
        om = attend(q_nope[:, :N_META], q_rope[:, :N_META]).reshape(bsz, N_META, B_HEADS * V_HEAD)
        o = jnp.concatenate([om, o], axis=1)
    return o


def peer(h, w_query, sub_keys1, sub_keys2, expert_down, expert_up):
    T = h.shape[0]
    pad = (-T) % PEER_CHUNK
    hp = jnp.pad(h, ((0, pad), (0, 0))).reshape(-1, PEER_CHUNK, D_MODEL)
    half = PEER_KEY_DIM // 2

    def chunk(hc):
        q = (hc @ w_query).reshape(PEER_CHUNK, PEER_HEADS, 2, half)
        s1 = jnp.einsum('chd,nd->chn', q[:, :, 0], sub_keys1).astype(jnp.float32)
        s2 = jnp.einsum('chd,nd->chn', q[:, :, 1], sub_keys2).astype(jnp.float32)
        v1, i1 = lax.top_k(s1, PEER_TOPK)
        v2, i2 = lax.top_k(s2, PEER_TOPK)
        cand = (v1[..., :, None] + v2[..., None, :]).reshape(PEER_CHUNK, PEER_HEADS, PEER_TOPK * PEER_TOPK)
        cand_idx = (i1[..., :, None] * N_KEYS + i2[..., None, :]).reshape(PEER_CHUNK, PEER_HEADS, PEER_TOPK * PEER_TOPK)
        sc, sel = lax.top_k(cand, PEER_TOPK)
        idx = jnp.take_along_axis(cand_idx, sel, axis=-1)
        g = jax.nn.softmax(sc, axis=-1)
        u = expert_down[idx]
        a = jax.nn.gelu(jnp.einsum('chkd,cd->chk', u, hc).astype(jnp.float32), approximate=False)
        w = (g * a).astype(hc.dtype)
        return jnp.einsum('chk,chkd->cd', w, expert_up[idx])

    return lax.map(chunk, hp).reshape(-1, D_MODEL)[:T]


def encode(x, meta_tokens, rel_table, norm_mix, w_in, g_cq, w_uq, g_ckv, w_ukv, attn_sink,
           w_o_a, w_o_b, w_out, norm_ffn, w_query, sub_keys1, sub_keys2, expert_down, expert_up, g_final):
    bsz = x.shape[0]
    meta = jnp.broadcast_to(meta_tokens.astype(x.dtype)[None], (bsz, N_META, D_MODEL))
    h = jnp.concatenate([meta, x], axis=1)
    L = h.shape[1]
    pos = jnp.arange(L)
    for l in range(DEPTH):
        keep_meta = l < DEPTH - 1
        q_off = 0 if keep_meta else N_META
        hn = rmsnorm(h, norm_mix[l])
        z = hn @ w_in[l]
        qa, ka, va, cq, ckv, kr, gates = jnp.split(z, IN_SPLITS, axis=-1)
        y_a = window_sink_attention(qa.reshape(bsz, L, A_HEADS, A_HEAD_DIM),
                                    ka.reshape(bsz, L, A_KV_HEADS, A_HEAD_DIM),
                                    va.reshape(bsz, L, A_KV_HEADS, A_HEAD_DIM),
                                    attn_sink[l], rel_table, keep_meta)
        qb = (rmsnorm(cq, g_cq[l]) @ w_uq[l]).reshape(bsz, L, B_HEADS, QK_NOPE + QK_ROPE)
        kvb = (rmsnorm(ckv, g_ckv[l]) @ w_ukv[l]).reshape(bsz, L, B_HEADS, QK_NOPE + V_HEAD)
        k_rope = rope(kr.reshape(bsz, L, 1, QK_ROPE), pos)[:, :, 0]
        y_b = mla_attention(qb[..., :QK_NOPE], rope(qb[..., QK_NOPE:], pos),
                            kvb[..., :QK_NOPE], k_rope, kvb[..., QK_NOPE:], keep_meta)
        g_a, g_b = jnp.split(gates[:, q_off:], 2, axis=-1)
        merged = jax.nn.sigmoid(g_a) * (y_a @ w_o_a[l]) + jax.nn.sigmoid(g_b) * (y_b @ w_o_b[l])
        h = h[:, q_off:] + merged @ w_out[l]
        hf = rmsnorm(h, norm_ffn[l])
        h = h + peer(hf.reshape(-1, D_MODEL), w_query[l], sub_keys1[l], sub_keys2[l],
                     expert_down[l], expert_up[l]).reshape(h.shape)
    return rmsnorm(h, g_final)


def setup_inputs(seed: int = 0) -> dict:
    key = jax.random.key(seed)
    ks = jax.random.split(key, 24)
    f32 = jnp.float32

    def nrm(k, shape, scale):
        return jax.random.normal(k, shape, f32) * scale

    def gain(k, shape):
        return 1.0 + 0.01 * jax.random.normal(k, shape, f32)

    return {
        "x_prompt": nrm(ks[0], (BATCH, SEQ, D_MODEL), 1.0),
        "x_sample": nrm(ks[1], (DEC_BATCH, DEC_SEQ, D_MODEL), 1.0),
        "meta_tokens": nrm(ks[2], (N_META, D_MODEL), 1.0),
        "rel_table": nrm(ks[3], (REL_BUCKETS, A_HEADS), 0.5),
        "norm_mix": gain(ks[4], (DEPTH, D_MODEL)),
        "w_in": nrm(ks[5], (DEPTH, D_MODEL, D_IN), D_MODEL ** -0.5),
        "g_cq": gain(ks[6], (DEPTH, Q_LORA)),
        "w_uq": nrm(ks[7], (DEPTH, Q_LORA, B_HEADS * (QK_NOPE + QK_ROPE)), Q_LORA ** -0.5),
        "g_ckv": gain(ks[8], (DEPTH, KV_LORA)),
        "w_ukv": nrm(ks[9], (DEPTH, KV_LORA, B_HEADS * (QK_NOPE + V_HEAD)), KV_LORA ** -0.5),
        "attn_sink": nrm(ks[10], (DEPTH, A_HEADS), 1.0),
        "w_o_a": nrm(ks[11], (DEPTH, A_Q_COLS, D_MODEL), A_Q_COLS ** -0.5),
        "w_o_b": nrm(ks[12], (DEPTH, B_OUT_COLS, D_MODEL), B_OUT_COLS ** -0.5),
        "w_out": nrm(ks[13], (DEPTH, D_MODEL, D_MODEL), D_MODEL ** -0.5),
        "norm_ffn": gain(ks[14], (DEPTH, D_MODEL)),
        "w_query": nrm(ks[15], (DEPTH, D_MODEL, PEER_HEADS * PEER_KEY_DIM), D_MODEL ** -0.5),
        "sub_keys1": nrm(ks[16], (DEPTH, N_KEYS, PEER_KEY_DIM // 2), (PEER_KEY_DIM // 2) ** -0.5),
        "sub_keys2": nrm(ks[17], (DEPTH, N_KEYS, PEER_KEY_DIM // 2), (PEER_KEY_DIM // 2) ** -0.5),
        "expert_down": nrm(ks[18], (DEPTH, N_EXPERTS, D_MODEL), D_MODEL ** -0.5),
        "expert_up": nrm(ks[19], (DEPTH, N_EXPERTS, D_MODEL), PEER_HEADS ** -0.5),
        "g_final": gain(ks[20], (D_MODEL,)),
    }


def reference(x_prompt, x_sample, meta_tokens, rel_table, norm_mix, w_in, g_cq, w_uq, g_ckv, w_ukv,
              attn_sink, w_o_a, w_o_b, w_out, norm_ffn, w_query, sub_keys1, sub_keys2,
              expert_down, expert_up, g_final):
    y_prompt = encode(x_prompt, meta_tokens, rel_table, norm_mix, w_in, g_cq, w_uq, g_ckv, w_ukv, attn_sink,
                      w_o_a, w_o_b, w_out, norm_ffn, w_query, sub_keys1, sub_keys2, expert_down, expert_up, g_final)
    y_sample = encode(x_sample, meta_tokens, rel_table, norm_mix, w_in, g_cq, w_uq, g_ckv, w_ukv, attn_sink,
                      w_o_a, w_o_b, w_out, norm_ffn, w_query, sub_keys1, sub_keys2, expert_down, expert_up, g_final)
    return (y_prompt, y_sample)
```

```python
import functools
import math

import jax
import jax.numpy as jnp
from jax import lax
from jax.experimental import pallas as pl
from jax.experimental.pallas import tpu as pltpu

F32 = jnp.float32
BF16 = jnp.bfloat16

D_MODEL = 2048
N_META = 16
BLOCK = 128
WINDOW = 128
A_HEADS = 16
A_KV_HEADS = 4
A_HEAD_DIM = 128
A_GROUP = A_HEADS // A_KV_HEADS
B_HEADS = 16
Q_LORA = 512
KV_LORA = 256
QK_NOPE = 128
QK_ROPE = 64
V_HEAD = 128
ROPE_THETA = 10000.0
REL_BUCKETS = 32
REL_MAX_DIST = 128
PEER_HEADS = 8
N_KEYS = 128
N_EXPERTS = N_KEYS * N_KEYS
PEER_KEY_DIM = 128
PEER_TOPK = 16
EPS = 1e-6
NEG = -1e30

A_Q_COLS = A_HEADS * A_HEAD_DIM
A_KV_COLS = A_KV_HEADS * A_HEAD_DIM
QKV_COLS = A_Q_COLS + 2 * A_KV_COLS
LAT_COLS = 896
QK_PAD = 256
LANES = 128
META_PAD = 128
VMEM_LIMIT = 56 * 1024 * 1024

_NT = (((1,), (1,)), ((), ()))
_TN = (((0,), (0,)), ((), ()))


def _params(sem):
    return pltpu.CompilerParams(dimension_semantics=sem, vmem_limit_bytes=VMEM_LIMIT)


def _pick(n, pref):
    if n <= pref:
        return n
    t = pref
    while n % t:
        t //= 2
    return t


def _rms(xf, g):
    ms = jnp.mean(xf * xf, axis=-1, keepdims=True)
    return xf * lax.rsqrt(ms + EPS) * g


def _norm_matmul_kernel(x_ref, g_ref, w_ref, o_ref, xn_ref):
    @pl.when(pl.program_id(1) == 0)
    def _():
        xn_ref[...] = _rms(x_ref[...].astype(F32), g_ref[...]).astype(BF16)

    o_ref[...] = jnp.dot(xn_ref[...], w_ref[...], preferred_element_type=F32).astype(o_ref.dtype)


def norm_matmul(x, g, w, out_dtype, tm_pref=512, tn_pref=512):
    m, k = x.shape
    n = w.shape[1]
    tm, tn = _pick(m, tm_pref), _pick(n, tn_pref)
    return pl.pallas_call(
        _norm_matmul_kernel,
        grid=(m // tm, n // tn),
        in_specs=[
            pl.BlockSpec((tm, k), lambda i, j: (i, 0)),
            pl.BlockSpec((1, k), lambda i, j: (0, 0)),
            pl.BlockSpec((k, tn), lambda i, j: (0, j)),
        ],
        out_specs=pl.BlockSpec((tm, tn), lambda i, j: (i, j)),
        out_shape=jax.ShapeDtypeStruct((m, n), out_dtype),
        scratch_shapes=[pltpu.VMEM((tm, k), BF16)],
        compiler_params=_params(("parallel", "arbitrary")),
        name="norm_matmul",
    )(x, g.reshape(1, k).astype(F32), w)


def _rope128(xg, cos, sin_lo, sin_hi):
    return xg * cos + pltpu.roll(xg, 96, 1) * sin_lo + pltpu.roll(xg, 32, 1) * sin_hi


def _mla_proj_kernel(z_ref, gq_ref, gkv_ref, wq_ref, wk_ref, wv_ref, cos_ref, slo_ref, shi_ref,
                     q_ref, k_ref, v_ref, *, q_scale):
    z = z_ref[...]
    nq = _rms(z[:, :Q_LORA], gq_ref[...]).astype(BF16)
    nkv = _rms(z[:, Q_LORA:Q_LORA + KV_LORA], gkv_ref[...]).astype(BF16)
    cos, slo, shi = cos_ref[...], slo_ref[...], shi_ref[...]
    kr = _rope128(z[:, Q_LORA + KV_LORA:], cos, slo, shi).astype(BF16)
    for h in range(B_HEADS):
        qh = jnp.dot(nq, wq_ref[:, h * QK_PAD:(h + 1) * QK_PAD], preferred_element_type=F32)
        q_ref[:, h * QK_PAD:h * QK_PAD + LANES] = (qh[:, :LANES] * q_scale).astype(BF16)
        q_ref[:, h * QK_PAD + LANES:(h + 1) * QK_PAD] = (
            _rope128(qh[:, LANES:], cos, slo, shi) * q_scale).astype(BF16)
        kh = jnp.dot(nkv, wk_ref[:, h * QK_NOPE:(h + 1) * QK_NOPE], preferred_element_type=F32)
        k_ref[:, h * QK_PAD:h * QK_PAD + LANES] = kh.astype(BF16)
        k_ref[:, h * QK_PAD + LANES:(h + 1) * QK_PAD] = kr
        vh = jnp.dot(nkv, wv_ref[:, h * V_HEAD:(h + 1) * V_HEAD], preferred_element_type=F32)
        v_ref[:, h * V_HEAD:(h + 1) * V_HEAD] = vh.astype(BF16)


def mla_proj(z_lat, g_cq, g_ckv, wq, wk, wv, rope_tabs, seq, tm_pref=256):
    t = z_lat.shape[0]
    tm = _pick(min(t, seq), tm_pref)
    nseq = seq // tm
    q_scale = (QK_NOPE + QK_ROPE) ** -0.5 * math.log2(math.e)
    full = lambda a: pl.BlockSpec(a.shape, lambda i: (0, 0))
    tab = pl.BlockSpec((tm, LANES), lambda i: (i % nseq, 0))
    gq, gkv = g_cq.reshape(1, -1).astype(F32), g_ckv.reshape(1, -1).astype(F32)
    return pl.pallas_call(
        functools.partial(_mla_proj_kernel, q_scale=q_scale),
        grid=(t // tm,),
        in_specs=[pl.BlockSpec((tm, LAT_COLS), lambda i: (i, 0)), full(gq), full(gkv),
                  full(wq), full(wk), full(wv), tab, tab, tab],
        out_specs=[pl.BlockSpec((tm, B_HEADS * QK_PAD), lambda i: (i, 0)),
                   pl.BlockSpec((tm, B_HEADS * QK_PAD), lambda i: (i, 0)),
                   pl.BlockSpec((tm, B_HEADS * V_HEAD), lambda i: (i, 0))],
        out_shape=[jax.ShapeDtypeStruct((t, B_HEADS * QK_PAD), BF16),
                   jax.ShapeDtypeStruct((t, B_HEADS * QK_PAD), BF16),
                   jax.ShapeDtypeStruct((t, B_HEADS * V_HEAD), BF16)],
        compiler_params=_params(("parallel",)),
        name="mla_proj",
    )(z_lat, gq, gkv, wq, wk, wv, *rope_tabs)


def _window_kernel(q_ref, kl_ref, kc_ref, kr_ref, vl_ref, vc_ref, vr_ref, km_ref, vm_ref,
                   bw_ref, bm_ref, o_ref, *, nblk):
    n = pl.program_id(1)
    scale = A_HEAD_DIM ** -0.5
    col = lax.broadcasted_iota(jnp.int32, (1, 3 * BLOCK), 1)
    left_pen = jnp.where(n > 0, 0.0, NEG).astype(F32)
    right_pen = jnp.where(n < nblk - 1, 0.0, NEG).astype(F32)
    edge = jnp.where(col < BLOCK, left_pen, 0.0) + jnp.where(col >= 2 * BLOCK, right_pen, 0.0)
    for kh in range(A_KV_HEADS):
        cs = slice(kh * A_HEAD_DIM, (kh + 1) * A_HEAD_DIM)
        kband = jnp.concatenate([kl_ref[:, cs], kc_ref[:, cs], kr_ref[:, cs]], axis=0)
        vband = jnp.concatenate([vl_ref[:, cs], vc_ref[:, cs], vr_ref[:, cs]], axis=0)
        q4 = jnp.concatenate(
            [q_ref[:, (kh * A_GROUP + g) * A_HEAD_DIM:(kh * A_GROUP + g + 1) * A_HEAD_DIM]
             for g in range(A_GROUP)], axis=0)
        s_w = lax.dot_general(q4, kband, _NT, preferred_element_type=F32) * scale + bw_ref[kh] + edge
        s_m = lax.dot_general(q4, km_ref[:, cs], _NT, preferred_element_type=F32) * scale + bm_ref[0, kh]
        m = jnp.maximum(jnp.max(s_w, axis=-1, keepdims=True), jnp.max(s_m, axis=-1, keepdims=True))
        p_w = jnp.exp(s_w - m)
        p_m = jnp.exp(s_m - m)
        denom = jnp.sum(p_w, axis=-1, keepdims=True) + jnp.sum(p_m, axis=-1, keepdims=True)
        o = (jnp.dot(p_w.astype(BF16), vband, preferred_element_type=F32)
             + jnp.dot(p_m.astype(BF16), vm_ref[:, cs], preferred_element_type=F32)) / denom
        for g in range(A_GROUP):
            h = kh * A_GROUP + g
            o_ref[:, h * A_HEAD_DIM:(h + 1) * A_HEAD_DIM] = o[g * BLOCK:(g + 1) * BLOCK].astype(o_ref.dtype)


def window_attention(z_qkv, zm_qkv, bias_w, bias_m, bsz, seq):
    nblk = seq // BLOCK
    kcol, vcol = A_Q_COLS // A_KV_COLS, A_Q_COLS // A_KV_COLS + 1
    row = lambda b, n: b * nblk + n
    lo = lambda b, n: b * nblk + jnp.maximum(n - 1, 0)
    hi = lambda b, n: b * nblk + jnp.minimum(n + 1, nblk - 1)
    kv = lambda r, c: pl.BlockSpec((BLOCK, A_KV_COLS), lambda b, n: (r(b, n), c))
    return pl.pallas_call(
        functools.partial(_window_kernel, nblk=nblk),
        grid=(bsz, nblk),
        in_specs=[
            pl.BlockSpec((BLOCK, A_Q_COLS), lambda b, n: (row(b, n), 0)),
            kv(lo, kcol), kv(row, kcol), kv(hi, kcol),
            kv(lo, vcol), kv(row, vcol), kv(hi, vcol),
            pl.BlockSpec((META_PAD, A_KV_COLS), lambda b, n: (0, kcol)),
            pl.BlockSpec((META_PAD, A_KV_COLS), lambda b, n: (0, vcol)),
            pl.BlockSpec(bias_w.shape, lambda b, n: (0, 0, 0)),
            pl.BlockSpec((1,) + bias_m.shape[1:], lambda b, n: (jnp.minimum(n, 1), 0, 0, 0)),
        ],
        out_specs=pl.BlockSpec((BLOCK, A_Q_COLS), lambda b, n: (row(b, n), 0)),
        out_shape=jax.ShapeDtypeStruct((bsz * seq, A_Q_COLS), BF16),
        compiler_params=_params(("parallel", "arbitrary")),
        name="window_attention",
    )(z_qkv, z_qkv, z_qkv, z_qkv, z_qkv, z_qkv, z_qkv, zm_qkv, zm_qkv, bias_w, bias_m)


def _mla_kernel(q_ref, k_ref, v_ref, km_ref, vm_ref, o_ref, acc_ref, *, tk, nchunk):
    q = q_ref[...]
    s0 = lax.dot_general(q, km_ref[...], _NT, preferred_element_type=F32)
    mcol = lax.broadcasted_iota(jnp.int32, (1, META_PAD), 1)
    s0 = jnp.where(mcol < N_META, s0, NEG)
    m0 = jnp.max(s0, axis=-1, keepdims=True)
    p0 = jnp.exp2(s0 - m0)
    l0 = jnp.sum(p0, axis=-1, keepdims=True)
    acc_ref[...] = jnp.dot(p0.astype(BF16), vm_ref[...], preferred_element_type=F32)

    def body(c, carry):
        m, l = carry
        start = pl.multiple_of(c * tk, tk)
        s = lax.dot_general(q, k_ref[pl.ds(start, tk), :], _NT, preferred_element_type=F32)
        m_new = jnp.maximum(m, jnp.max(s, axis=-1, keepdims=True))
        alpha = jnp.exp2(m - m_new)
        p = jnp.exp2(s - m_new)
        l_new = alpha * l + jnp.sum(p, axis=-1, keepdims=True)
        acc_ref[...] = alpha * acc_ref[...] + jnp.dot(
            p.astype(BF16), v_ref[pl.ds(start, tk), :], preferred_element_type=F32)
        return m_new, l_new

    _, l = lax.fori_loop(0, nchunk, body, (m0, l0))
    o_ref[...] = (acc_ref[...] / l).astype(o_ref.dtype)


def mla_attention(q, k, v, km, vm, bsz, seq, tq_pref=512, tk_pref=512):
    tq, tk = _pick(seq, tq_pref), _pick(seq, tk_pref)
    nq = seq // tq
    return pl.pallas_call(
        functools.partial(_mla_kernel, tk=tk, nchunk=seq // tk),
        grid=(bsz, B_HEADS, nq),
        in_specs=[
            pl.BlockSpec((tq, QK_PAD), lambda b, h, i: (b * nq + i, h)),
            pl.BlockSpec((seq, QK_PAD), lambda b, h, i: (b, h)),
            pl.BlockSpec((seq, V_HEAD), lambda b, h, i: (b, h)),
            pl.BlockSpec((META_PAD, QK_PAD), lambda b, h, i: (0, h)),
            pl.BlockSpec((META_PAD, V_HEAD), lambda b, h, i: (0, h)),
        ],
        out_specs=pl.BlockSpec((tq, V_HEAD), lambda b, h, i: (b * nq + i, h)),
        out_shape=jax.ShapeDtypeStruct((bsz * seq, B_HEADS * V_HEAD), BF16),
        scratch_shapes=[pltpu.VMEM((tq, V_HEAD), F32)],
        compiler_params=_params(("parallel", "parallel", "arbitrary")),
        name="mla_attention",
    )(q, k, v, km, vm)


def _merge_kernel(ya_ref, yb_ref, wa_ref, wb_ref, ga_ref, gb_ref, o_ref):
    pa = jnp.dot(ya_ref[...], wa_ref[...], preferred_element_type=F32)
    pb = jnp.dot(yb_ref[...], wb_ref[...], preferred_element_type=F32)
    o_ref[...] = (jax.nn.sigmoid(ga_ref[...].astype(F32)) * pa
                  + jax.nn.sigmoid(gb_ref[...].astype(F32)) * pb).astype(o_ref.dtype)


def gated_merge(ya, yb, w_o_a, w_o_b, gates, tm_pref=512, tn_pref=512):
    t = ya.shape[0]
    tm, tn = _pick(t, tm_pref), _pick(D_MODEL, tn_pref)
    nj = D_MODEL // tn
    return pl.pallas_call(
        _merge_kernel,
        grid=(t // tm, nj),
        in_specs=[
            pl.BlockSpec((tm, A_Q_COLS), lambda i, j: (i, 0)),
            pl.BlockSpec((tm, B_HEADS * V_HEAD), lambda i, j: (i, 0)),
            pl.BlockSpec((A_Q_COLS, tn), lambda i, j: (0, j)),
            pl.BlockSpec((B_HEADS * V_HEAD, tn), lambda i, j: (0, j)),
            pl.BlockSpec((tm, tn), lambda i, j: (i, j)),
            pl.BlockSpec((tm, tn), lambda i, j: (i, j + nj)),
        ],
        out_specs=pl.BlockSpec((tm, tn), lambda i, j: (i, j)),
        out_shape=jax.ShapeDtypeStruct((t, D_MODEL), BF16),
        compiler_params=_params(("parallel", "arbitrary")),
        name="gated_merge",
    )(ya, yb, w_o_a, w_o_b, gates, gates)


def _out_proj_kernel(a_ref, w_ref, x_ref, o_ref):
    o_ref[...] = x_ref[...] + jnp.dot(a_ref[...], w_ref[...], preferred_element_type=F32)


def out_proj_residual(a, w, x, tm_pref=512, tn_pref=512):
    t, k = a.shape
    n = w.shape[1]
    tm, tn = _pick(t, tm_pref), _pick(n, tn_pref)
    return pl.pallas_call(
        _out_proj_kernel,
        grid=(t // tm, n // tn),
        in_specs=[
            pl.BlockSpec((tm, k), lambda i, j: (i, 0)),
            pl.BlockSpec((k, tn), lambda i, j: (0, j)),
            pl.BlockSpec((tm, tn), lambda i, j: (i, j)),
        ],
        out_specs=pl.BlockSpec((tm, tn), lambda i, j: (i, j)),
        out_shape=jax.ShapeDtypeStruct((t, n), F32),
        compiler_params=_params(("parallel", "arbitrary")),
        name="out_proj_residual",
    )(a, w, x)


def _top_values(s, k):
    rows = []
    for _ in range(k):
        m = jnp.max(s, axis=0, keepdims=True)
        rows.append(m)
        s = jnp.where(s == m, -jnp.inf, s)
    return rows


def _peer_route_kernel(h_ref, g_ref, wq_ref, sk1_ref, sk2_ref,
                       hf_ref, s1_ref, r1_ref, s2_ref, e2_ref, tau_ref, qt_ref):
    hf = _rms(h_ref[...], g_ref[...]).astype(BF16)
    hf_ref[...] = hf
    qt_ref[...] = lax.dot_general(wq_ref[...], hf, _NT, preferred_element_type=F32).astype(BF16)
    half = PEER_KEY_DIM // 2
    tc = hf.shape[0]
    row16 = lax.broadcasted_iota(jnp.int32, (PEER_TOPK, tc), 0)

    def head(h, carry):
        base = pl.multiple_of(h * PEER_KEY_DIM, PEER_KEY_DIM)
        q1 = qt_ref[pl.ds(base, half), :]
        q2 = qt_ref[pl.ds(base + half, half), :]
        s1 = jnp.dot(sk1_ref[...], q1, preferred_element_type=F32)
        s2 = jnp.dot(sk2_ref[...], q2, preferred_element_type=F32)
        v1 = _top_values(s1, PEER_TOPK)
        v2_rows = _top_values(s2, PEER_TOPK)
        v2 = jnp.zeros((PEER_TOPK, tc), F32)
        for r, row in enumerate(v2_rows):
            v2 = jnp.where(row16 == r, row, v2)
        cand = jnp.concatenate([v1[a] + v2 for a in range(PEER_TOPK)], axis=0)
        rem = cand
        tau = jnp.zeros((1, tc), F32)
        cnt = jnp.zeros((1, tc), F32)
        for _ in range(PEER_TOPK):
            m = jnp.max(rem, axis=0, keepdims=True)
            hit = rem == m
            tau = jnp.where(cnt < PEER_TOPK, m, tau)
            cnt = cnt + jnp.sum(hit.astype(F32), axis=0, keepdims=True)
            rem = jnp.where(hit, -jnp.inf, rem)
        m1, m2 = v1[0], v2_rows[0]
        z = jnp.sum(jnp.where(cand >= tau, jnp.exp(cand - (m1 + m2)), 0.0), axis=0, keepdims=True)
        s1_ref[h] = s1
        r1_ref[h] = jnp.exp(s1 - m1) / z
        s2_ref[h] = s2
        e2_ref[h] = jnp.exp(s2 - m2)
        tau_ref[h] = tau
        return carry

    lax.fori_loop(0, PEER_HEADS, head, 0)


def peer_route(h1, norm_ffn, wq_t, sk1, sk2, tc_pref=256):
    t = h1.shape[0]
    tc = _pick(t, tc_pref)
    full = lambda a: pl.BlockSpec(a.shape, lambda i: (0,) * a.ndim)
    g = norm_ffn.reshape(1, -1).astype(F32)
    keyed = pl.BlockSpec((PEER_HEADS, N_KEYS, tc), lambda i: (0, 0, i))
    keyed_shape = jax.ShapeDtypeStruct((PEER_HEADS, N_KEYS, t), F32)
    return pl.pallas_call(
        _peer_route_kernel,
        grid=(t // tc,),
        in_specs=[pl.BlockSpec((tc, D_MODEL), lambda i: (i, 0)), full(g), full(wq_t), full(sk1), full(sk2)],
        out_specs=[pl.BlockSpec((tc, D_MODEL), lambda i: (i, 0)), keyed, keyed, keyed, keyed,
                   pl.BlockSpec((PEER_HEADS, 1, tc), lambda i: (0, 0, i))],
        out_shape=[jax.ShapeDtypeStruct((t, D_MODEL), BF16), keyed_shape, keyed_shape, keyed_shape,
                   keyed_shape, jax.ShapeDtypeStruct((PEER_HEADS, 1, t), F32)],
        scratch_shapes=[pltpu.VMEM((PEER_HEADS * PEER_KEY_DIM, tc), BF16)],
        compiler_params=_params(("parallel",)),
        name="peer_route",
    )(h1, g, wq_t, sk1, sk2)


def _peer_expert_kernel(hf_ref, dn_ref, up_ref, s1_ref, r1_ref, s2_ref, e2_ref, tau_ref,
                        h1_ref, gf_ref, o_ref, acc_ref, wt_ref, *, ib, ne):
    e = pl.program_id(1)

    @pl.when(e == 0)
    def _():
        acc_ref[...] = jnp.zeros_like(acc_ref)

    tc = hf_ref.shape[0]
    at = lax.dot_general(dn_ref[...], hf_ref[...], _NT, preferred_element_type=F32)
    act = 0.5 * at * (1.0 + lax.erf(at * (2.0 ** -0.5)))
    for il in range(ib):
        for lc in range(tc // LANES):
            ls = slice(lc * LANES, (lc + 1) * LANES)
            g = jnp.zeros((N_KEYS, LANES), F32)
            for h in range(PEER_HEADS):
                ssum = s1_ref[h, il:il + 1, ls] + s2_ref[h, :, ls]
                w = e2_ref[h, :, ls] * r1_ref[h, il:il + 1, ls]
                g = g + jnp.where(ssum >= tau_ref[h, :, ls], w, 0.0)
            wt_ref[il * N_KEYS:(il + 1) * N_KEYS, ls] = (
                act[il * N_KEYS:(il + 1) * N_KEYS, ls] * g).astype(BF16)
    acc_ref[...] += lax.dot_general(wt_ref[...], up_ref[...], _TN, preferred_element_type=F32)

    @pl.when(e == ne - 1)
    def _():
        o_ref[...] = _rms(h1_ref[...] + acc_ref[...], gf_ref[...])


def peer_experts(hf, down, up, s1, r1, s2, e2, tau, h1, g_final, tc_pref=256, eb=1024):
    t = hf.shape[0]
    tc = _pick(t, tc_pref)
    ib = eb // N_KEYS
    ne = N_EXPERTS // eb
    gf = g_final.reshape(1, -1).astype(F32)
    blk_i = pl.BlockSpec((PEER_HEADS, ib, tc), lambda c, e: (0, e, c))
    blk_j = pl.BlockSpec((PEER_HEADS, N_KEYS, tc), lambda c, e: (0, 0, c))
    return pl.pallas_call(
        functools.partial(_peer_expert_kernel, ib=ib, ne=ne),
        grid=(t // tc, ne),
        in_specs=[
            pl.BlockSpec((tc, D_MODEL), lambda c, e: (c, 0)),
            pl.BlockSpec((eb, D_MODEL), lambda c, e: (e, 0)),
            pl.BlockSpec((eb, D_MODEL), lambda c, e: (e, 0)),
            blk_i, blk_i, blk_j, blk_j,
            pl.BlockSpec((PEER_HEADS, 1, tc), lambda c, e: (0, 0, c)),
            pl.BlockSpec((tc, D_MODEL), lambda c, e: (c, 0)),
            pl.BlockSpec((1, D_MODEL), lambda c, e: (0, 0)),
        ],
        out_specs=pl.BlockSpec((tc, D_MODEL), lambda c, e: (c, 0)),
        out_shape=jax.ShapeDtypeStruct((t, D_MODEL), F32),
        scratch_shapes=[pltpu.VMEM((tc, D_MODEL), F32), pltpu.VMEM((eb, tc), BF16)],
        compiler_params=_params(("parallel", "arbitrary")),
        name="peer_experts",
    )(hf, down, up, s1, r1, s2, e2, tau, h1, gf)


def _rel_bucket(rel):
    nb = REL_BUCKETS // 2
    max_exact = nb // 2
    ret = jnp.where(rel > 0, nb, 0)
    n = jnp.abs(rel)
    nf = jnp.maximum(n, 1).astype(F32)
    large = max_exact + (jnp.log(nf / max_exact) / math.log(REL_MAX_DIST / max_exact)
                         * (nb - max_exact)).astype(jnp.int32)
    large = jnp.minimum(large, nb - 1)
    return ret + jnp.where(n < max_exact, n, large)


def _window_bias(rel_table, sink):
    i = jnp.arange(BLOCK)
    c = jnp.arange(3 * BLOCK)
    rel_w = c[None, :] - BLOCK - i[:, None]
    bw = rel_table[_rel_bucket(rel_w)].astype(F32).transpose(2, 0, 1)
    bw = jnp.where((jnp.abs(rel_w) <= WINDOW)[None], bw, NEG)
    bw = bw.reshape(A_KV_HEADS, A_GROUP * BLOCK, 3 * BLOCK)
    q_pos = N_META + jnp.arange(2)[:, None] * BLOCK + i[None, :]
    rel_m = jnp.arange(N_META)[None, None, :] - q_pos[:, :, None]
    bm = rel_table[_rel_bucket(rel_m)].astype(F32).transpose(0, 3, 1, 2)
    sink_col = jnp.broadcast_to(sink.astype(F32)[None, :, None, None], (2, A_HEADS, BLOCK, 1))
    pad = jnp.full((2, A_HEADS, BLOCK, META_PAD - N_META - 1), NEG, F32)
    bm = jnp.concatenate([bm, sink_col, pad], axis=-1)
    return bw, bm.reshape(2, A_KV_HEADS, A_GROUP * BLOCK, META_PAD)


def _rope_tables(length):
    half = QK_ROPE // 2
    freqs = ROPE_THETA ** (-jnp.arange(half, dtype=F32) / half)
    ang = jnp.arange(length).astype(F32)[:, None] * freqs[None, :]
    c, s = jnp.cos(ang), jnp.sin(ang)
    z = jnp.zeros((length, LANES - QK_ROPE), F32)
    zh = jnp.zeros((length, half), F32)
    return (jnp.concatenate([c, c, z], axis=1),
            jnp.concatenate([-s, zh, z], axis=1),
            jnp.concatenate([zh, s, z], axis=1))


def _pad_rows(a, rows):
    return jnp.pad(a, ((0, rows - a.shape[0]), (0, 0)))


def kernel(x_prompt, x_sample, meta_tokens, rel_table, norm_mix, w_in, g_cq, w_uq, g_ckv, w_ukv, attn_sink,
           w_o_a, w_o_b, w_out, norm_ffn, w_query, sub_keys1, sub_keys2, expert_down, expert_up, g_final):
    assert w_in.shape[0] == 1, "single-layer block"
    wi = w_in[0]
    c1 = QKV_COLS
    c2 = c1 + Q_LORA + KV_LORA + QK_ROPE
    w_qkv = wi[:, :c1].astype(BF16)
    w_lat = jnp.pad(wi[:, c1:c2], ((0, 0), (0, LAT_COLS - (c2 - c1)))).astype(BF16)
    w_gate = wi[:, c2:].astype(BF16)
    wq = jnp.pad(w_uq[0].reshape(Q_LORA, B_HEADS, QK_NOPE + QK_ROPE),
                 ((0, 0), (0, 0), (0, QK_PAD - QK_NOPE - QK_ROPE))).reshape(Q_LORA, B_HEADS * QK_PAD).astype(BF16)
    wkv = w_ukv[0].reshape(KV_LORA, B_HEADS, QK_NOPE + V_HEAD)
    wk = wkv[:, :, :QK_NOPE].reshape(KV_LORA, B_HEADS * QK_NOPE).astype(BF16)
    wv = wkv[:, :, QK_NOPE:].reshape(KV_LORA, B_HEADS * V_HEAD).astype(BF16)
    woa, wob, wout = w_o_a[0].astype(BF16), w_o_b[0].astype(BF16), w_out[0].astype(BF16)
    wq_t = w_query[0].T.astype(BF16)
    sk1, sk2 = sub_keys1[0].astype(BF16), sub_keys2[0].astype(BF16)
    down, up = expert_down[0].astype(BF16), expert_up[0].astype(BF16)
    bias_w, bias_m = _window_bias(rel_table, attn_sink[0])

    zm_qkv = norm_matmul(meta_tokens, norm_mix[0], w_qkv, BF16)
    zm_lat = norm_matmul(meta_tokens, norm_mix[0], w_lat, F32)
    _, km, vm = mla_proj(zm_lat, g_cq[0], g_ckv[0], wq, wk, wv, _rope_tables(N_META), N_META)
    zm_qkv, km, vm = _pad_rows(zm_qkv, META_PAD), _pad_rows(km, META_PAD), _pad_rows(vm, META_PAD)

    def encode(x):
        bsz, seq, _ = x.shape
        xt = x.reshape(bsz * seq, D_MODEL)
        tabs = tuple(t[N_META:] for t in _rope_tables(N_META + seq))
        z_qkv = norm_matmul(xt, norm_mix[0], w_qkv, BF16)
        z_lat = norm_matmul(xt, norm_mix[0], w_lat, F32)
        gates = norm_matmul(xt, norm_mix[0], w_gate, BF16)
        q, k, v = mla_proj(z_lat, g_cq[0], g_ckv[0], wq, wk, wv, tabs, seq)
        y_a = window_attention(z_qkv, zm_qkv, bias_w, bias_m, bsz, seq)
        y_b = mla_attention(q, k, v, km, vm, bsz, seq)
        merged = gated_merge(y_a, y_b, woa, wob, gates)
        h1 = out_proj_residual(merged, wout, xt)
        hf, s1, r1, s2, e2, tau = peer_route(h1, norm_ffn[0], wq_t, sk1, sk2)
        y = peer_experts(hf, down, up, s1, r1, s2, e2, tau, h1, g_final)
        return y.reshape(bsz, seq, D_MODEL)

    return encode(x_prompt), encode(x_sample)
```

```python
import functools
import math

import jax
import jax.numpy as jnp
from jax import lax
from jax.experimental import pallas as pl
from jax.experimental.pallas import tpu as pltpu

F32 = jnp.float32
BF16 = jnp.bfloat16

D_MODEL = 2048
N_META = 16
BLOCK = 128
WINDOW = 128
A_HEADS = 16
A_KV_HEADS = 4
A_HEAD_DIM = 128
A_GROUP = A_HEADS // A_KV_HEADS
B_HEADS = 16
Q_LORA = 512
KV_LORA = 256
QK_NOPE = 128
QK_ROPE = 64
V_HEAD = 128
ROPE_THETA = 10000.0
REL_BUCKETS = 32
REL_MAX_DIST = 128
PEER_HEADS = 8
N_KEYS = 128
N_EXPERTS = N_KEYS * N_KEYS
PEER_KEY_DIM = 128
PEER_TOPK = 16
EPS = 1e-6
NEG = -1e30

A_Q_COLS = A_HEADS * A_HEAD_DIM
A_KV_COLS = A_KV_HEADS * A_HEAD_DIM
QKV_COLS = A_Q_COLS + 2 * A_KV_COLS
LAT_COLS = 896
QK_PAD = 256
LANES = 128
META_PAD = 128
VMEM_LIMIT = 56 * 1024 * 1024

_NT = (((1,), (1,)), ((), ()))
_TN = (((0,), (0,)), ((), ()))


def _params(sem):
    return pltpu.CompilerParams(dimension_semantics=sem, vmem_limit_bytes=VMEM_LIMIT)


def _pick(n, pref):
    if n <= pref:
        return n
    t = pref
    while n % t:
        t //= 2
    return t


def _rms(xf, g):
    ms = jnp.mean(xf * xf, axis=-1, keepdims=True)
    return xf * lax.rsqrt(ms + EPS) * g


def _norm_matmul_kernel(x_ref, g_ref, w_ref, o_ref, xn_ref):
    @pl.when(pl.program_id(1) == 0)
    def _():
        xn_ref[...] = _rms(x_ref[...].astype(F32), g_ref[...]).astype(BF16)

    o_ref[...] = jnp.dot(xn_ref[...], w_ref[...], preferred_element_type=F32).astype(o_ref.dtype)


def norm_matmul(x, g, w, out_dtype, tm_pref=512, tn_pref=512):
    m, k = x.shape
    n = w.shape[1]
    tm, tn = _pick(m, tm_pref), _pick(n, tn_pref)
    return pl.pallas_call(
        _norm_matmul_kernel,
        grid=(m // tm, n // tn),
        in_specs=[
            pl.BlockSpec((tm, k), lambda i, j: (i, 0)),
            pl.BlockSpec((1, k), lambda i, j: (0, 0)),
            pl.BlockSpec((k, tn), lambda i, j: (0, j)),
        ],
        out_specs=pl.BlockSpec((tm, tn), lambda i, j: (i, j)),
        out_shape=jax.ShapeDtypeStruct((m, n), out_dtype),
        scratch_shapes=[pltpu.VMEM((tm, k), BF16)],
        compiler_params=_params(("parallel", "arbitrary")),
        name="norm_matmul",
    )(x, g.reshape(1, k).astype(F32), w)


def _rope128(xg, cos, sin_lo, sin_hi):
    return xg * cos + pltpu.roll(xg, 96, 1) * sin_lo + pltpu.roll(xg, 32, 1) * sin_hi


def _mla_proj_kernel(z_ref, gq_ref, gkv_ref, wq_ref, wk_ref, wv_ref, cos_ref, slo_ref, shi_ref,
                     q_ref, k_ref, v_ref, *, q_scale):
    z = z_ref[...]
    nq = _rms(z[:, :Q_LORA], gq_ref[...]).astype(BF16)
    nkv = _rms(z[:, Q_LORA:Q_LORA + KV_LORA], gkv_ref[...]).astype(BF16)
    cos, slo, shi = cos_ref[...], slo_ref[...], shi_ref[...]
    kr = _rope128(z[:, Q_LORA + KV_LORA:], cos, slo, shi).astype(BF16)
    for h in range(B_HEADS):
        qh = jnp.dot(nq, wq_ref[:, h * QK_PAD:(h + 1) * QK_PAD], preferred_element_type=F32)
        q_ref[:, h * QK_PAD:h * QK_PAD + LANES] = (qh[:, :LANES] * q_scale).astype(BF16)
        q_ref[:, h * QK_PAD + LANES:(h + 1) * QK_PAD] = (
            _rope128(qh[:, LANES:], cos, slo, shi) * q_scale).astype(BF16)
        kh = jnp.dot(nkv, wk_ref[:, h * QK_NOPE:(h + 1) * QK_NOPE], preferred_element_type=F32)
        k_ref[:, h * QK_PAD:h * QK_PAD + LANES] = kh.astype(BF16)
        k_ref[:, h * QK_PAD + LANES:(h + 1) * QK_PAD] = kr
        vh = jnp.dot(nkv, wv_ref[:, h * V_HEAD:(h + 1) * V_HEAD], preferred_element_type=F32)
        v_ref[:, h * V_HEAD:(h + 1) * V_HEAD] = vh.astype(BF16)


def mla_proj(z_lat, g_cq, g_ckv, wq, wk, wv, rope_tabs, seq, tm_pref=256):
    t = z_lat.shape[0]
    tm = _pick(min(t, seq), tm_pref)
    nseq = seq // tm
    q_scale = (QK_NOPE + QK_ROPE) ** -0.5 * math.log2(math.e)
    full = lambda a: pl.BlockSpec(a.shape, lambda i: (0, 0))
    tab = pl.BlockSpec((tm, LANES), lambda i: (i % nseq, 0))
    gq, gkv = g_cq.reshape(1, -1).astype(F32), g_ckv.reshape(1, -1).astype(F32)
    return pl.pallas_call(
        functools.partial(_mla_proj_kernel, q_scale=q_scale),
        grid=(t // tm,),
        in_specs=[pl.BlockSpec((tm, LAT_COLS), lambda i: (i, 0)), full(gq), full(gkv),
                  full(wq), full(wk), full(wv), tab, tab, tab],
        out_specs=[pl.BlockSpec((tm, B_HEADS * QK_PAD), lambda i: (i, 0)),
                   pl.BlockSpec((tm, B_HEADS * QK_PAD), lambda i: (i, 0)),
                   pl.BlockSpec((tm, B_HEADS * V_HEAD), lambda i: (i, 0))],
        out_shape=[jax.ShapeDtypeStruct((t, B_HEADS * QK_PAD), BF16),
                   jax.ShapeDtypeStruct((t, B_HEADS * QK_PAD), BF16),
                   jax.ShapeDtypeStruct((t, B_HEADS * V_HEAD), BF16)],
        compiler_params=_params(("parallel",)),
        name="mla_proj",
    )(z_lat, gq, gkv, wq, wk, wv, *rope_tabs)


def _window_kernel(q_ref, kl_ref, kc_ref, kr_ref, vl_ref, vc_ref, vr_ref, km_ref, vm_ref,
                   bw_ref, bm_ref, o_ref, *, nblk):
    n = pl.program_id(1)
    scale = A_HEAD_DIM ** -0.5
    col = lax.broadcasted_iota(jnp.int32, (1, 3 * BLOCK), 1)
    left_pen = jnp.where(n > 0, 0.0, NEG).astype(F32)
    right_pen = jnp.where(n < nblk - 1, 0.0, NEG).astype(F32)
    edge = jnp.where(col < BLOCK, left_pen, 0.0) + jnp.where(col >= 2 * BLOCK, right_pen, 0.0)
    for kh in range(A_KV_HEADS):
        cs = slice(kh * A_HEAD_DIM, (kh + 1) * A_HEAD_DIM)
        kband = jnp.concatenate([kl_ref[:, cs], kc_ref[:, cs], kr_ref[:, cs]], axis=0)
        vband = jnp.concatenate([vl_ref[:, cs], vc_ref[:, cs], vr_ref[:, cs]], axis=0)
        q4 = jnp.concatenate(
            [q_ref[:, (kh * A_GROUP + g) * A_HEAD_DIM:(kh * A_GROUP + g + 1) * A_HEAD_DIM]
             for g in range(A_GROUP)], axis=0)
        s_w = lax.dot_general(q4, kband, _NT, preferred_element_type=F32) * scale + bw_ref[kh] + edge
        s_m = lax.dot_general(q4, km_ref[:, cs], _NT, preferred_element_type=F32) * scale + bm_ref[0, kh]
        m = jnp.maximum(jnp.max(s_w, axis=-1, keepdims=True), jnp.max(s_m, axis=-1, keepdims=True))
        p_w = jnp.exp(s_w - m)
        p_m = jnp.exp(s_m - m)
        denom = jnp.sum(p_w, axis=-1, keepdims=True) + jnp.sum(p_m, axis=-1, keepdims=True)
        o = (jnp.dot(p_w.astype(BF16), vband, preferred_element_type=F32)
             + jnp.dot(p_m.astype(BF16), vm_ref[:, cs], preferred_element_type=F32)) / denom
        for g in range(A_GROUP):
            h = kh * A_GROUP + g
            o_ref[:, h * A_HEAD_DIM:(h + 1) * A_HEAD_DIM] = o[g * BLOCK:(g + 1) * BLOCK].astype(o_ref.dtype)


def window_attention(z_qkv, zm_qkv, bias_w, bias_m, bsz, seq):
    nblk = seq // BLOCK
    kcol, vcol = A_Q_COLS // A_KV_COLS, A_Q_COLS // A_KV_COLS + 1
    row = lambda b, n: b * nblk + n
    lo = lambda b, n: b * nblk + jnp.maximum(n - 1, 0)
    hi = lambda b, n: b * nblk + jnp.minimum(n + 1, nblk - 1)
    kv = lambda r, c: pl.BlockSpec((BLOCK, A_KV_COLS), lambda b, n: (r(b, n), c))
    return pl.pallas_call(
        functools.partial(_window_kernel, nblk=nblk),
        grid=(bsz, nblk),
        in_specs=[
            pl.BlockSpec((BLOCK, A_Q_COLS), lambda b, n: (row(b, n), 0)),
            kv(lo, kcol), kv(row, kcol), kv(hi, kcol),
            kv(lo, vcol), kv(row, vcol), kv(hi, vcol),
            pl.BlockSpec((META_PAD, A_KV_COLS), lambda b, n: (0, kcol)),
            pl.BlockSpec((META_PAD, A_KV_COLS), lambda b, n: (0, vcol)),
            pl.BlockSpec(bias_w.shape, lambda b, n: (0, 0, 0)),
            pl.BlockSpec((1,) + bias_m.shape[1:], lambda b, n: (jnp.minimum(n, 1), 0, 0, 0)),
        ],
        out_specs=pl.BlockSpec((BLOCK, A_Q_COLS), lambda b, n: (row(b, n), 0)),
        out_shape=jax.ShapeDtypeStruct((bsz * seq, A_Q_COLS), BF16),
        compiler_params=_params(("parallel", "arbitrary")),
        name="window_attention",
    )(z_qkv, z_qkv, z_qkv, z_qkv, z_qkv, z_qkv, z_qkv, zm_qkv, zm_qkv, bias_w, bias_m)


def _mla_kernel(q_ref, k_ref, v_ref, km_ref, vm_ref, o_ref, acc_ref, sa_ref, sb_ref, pa_ref, pb_ref,
                *, tk, npair):
    last = 2 * npair - 1

    def scores(c):
        start = pl.multiple_of(c * tk, tk)
        return lax.dot_general(q_ref[...], k_ref[pl.ds(start, tk), :], _NT, preferred_element_type=F32)

    def weighted(p_ref, c):
        start = pl.multiple_of(c * tk, tk)
        return jnp.dot(p_ref[...], v_ref[pl.ds(start, tk), :], preferred_element_type=F32)

    def softmax(s_ref, p_ref, m, l):
        s = s_ref[...]
        m_new = jnp.maximum(m, jnp.max(s, axis=-1, keepdims=True))
        alpha = jnp.exp2(m - m_new)
        p = jnp.exp2(s - m_new)
        p_ref[...] = p.astype(BF16)
        return m_new, alpha * l + jnp.sum(p, axis=-1, keepdims=True), alpha

    s0 = lax.dot_general(q_ref[...], km_ref[...], _NT, preferred_element_type=F32)
    mcol = lax.broadcasted_iota(jnp.int32, (1, META_PAD), 1)
    s0 = jnp.where(mcol < N_META, s0, NEG)
    m0 = jnp.max(s0, axis=-1, keepdims=True)
    p0 = jnp.exp2(s0 - m0)
    l0 = jnp.sum(p0, axis=-1, keepdims=True)
    acc_ref[...] = jnp.dot(p0.astype(BF16), vm_ref[...], preferred_element_type=F32)
    sa_ref[...] = scores(0)
    pb_ref[...] = jnp.zeros_like(pb_ref)

    def body(j, carry):
        m, l, alpha_b = carry
        c0 = 2 * j
        sb_ref[...] = scores(c0 + 1)
        acc_ref[...] = alpha_b * acc_ref[...] + weighted(pb_ref, jnp.maximum(c0 - 1, 0))
        m, l, alpha_a = softmax(sa_ref, pa_ref, m, l)
        sa_ref[...] = scores(jnp.minimum(c0 + 2, last))
        acc_ref[...] = alpha_a * acc_ref[...] + weighted(pa_ref, c0)
        m, l, alpha_b = softmax(sb_ref, pb_ref, m, l)
        return m, l, alpha_b

    _, l, alpha_b = lax.fori_loop(0, npair, body, (m0, l0, jnp.ones_like(m0)))
    acc = alpha_b * acc_ref[...] + weighted(pb_ref, last)
    o_ref[...] = (acc / l).astype(o_ref.dtype)


def mla_attention(q, k, v, km, vm, bsz, seq, tq_pref=512, tk_pref=512):
    tq, tk = _pick(seq, tq_pref), _pick(seq // 2, tk_pref)
    nq = seq // tq
    return pl.pallas_call(
        functools.partial(_mla_kernel, tk=tk, npair=seq // (2 * tk)),
        grid=(bsz, B_HEADS, nq),
        in_specs=[
            pl.BlockSpec((tq, QK_PAD), lambda b, h, i: (b * nq + i, h)),
            pl.BlockSpec((seq, QK_PAD), lambda b, h, i: (b, h)),
            pl.BlockSpec((seq, V_HEAD), lambda b, h, i: (b, h)),
            pl.BlockSpec((META_PAD, QK_PAD), lambda b, h, i: (0, h)),
            pl.BlockSpec((META_PAD, V_HEAD), lambda b, h, i: (0, h)),
        ],
        out_specs=pl.BlockSpec((tq, V_HEAD), lambda b, h, i: (b * nq + i, h)),
        out_shape=jax.ShapeDtypeStruct((bsz * seq, B_HEADS * V_HEAD), BF16),
        scratch_shapes=[pltpu.VMEM((tq, V_HEAD), F32),
                        pltpu.VMEM((tq, tk), F32), pltpu.VMEM((tq, tk), F32),
                        pltpu.VMEM((tq, tk), BF16), pltpu.VMEM((tq, tk), BF16)],
        compiler_params=_params(("parallel", "parallel", "arbitrary")),
        name="mla_attention",
    )(q, k, v, km, vm)


def _merge_kernel(ya_ref, yb_ref, wa_ref, wb_ref, ga_ref, gb_ref, o_ref):
    pa = jnp.dot(ya_ref[...], wa_ref[...], preferred_element_type=F32)
    pb = jnp.dot(yb_ref[...], wb_ref[...], preferred_element_type=F32)
    o_ref[...] = (jax.nn.sigmoid(ga_ref[...].astype(F32)) * pa
                  + jax.nn.sigmoid(gb_ref[...].astype(F32)) * pb).astype(o_ref.dtype)


def gated_merge(ya, yb, w_o_a, w_o_b, gates, tm_pref=512, tn_pref=512):
    t = ya.shape[0]
    tm, tn = _pick(t, tm_pref), _pick(D_MODEL, tn_pref)
    nj = D_MODEL // tn
    return pl.pallas_call(
        _merge_kernel,
        grid=(t // tm, nj),
        in_specs=[
            pl.BlockSpec((tm, A_Q_COLS), lambda i, j: (i, 0)),
            pl.BlockSpec((tm, B_HEADS * V_HEAD), lambda i, j: (i, 0)),
            pl.BlockSpec((A_Q_COLS, tn), lambda i, j: (0, j)),
            pl.BlockSpec((B_HEADS * V_HEAD, tn), lambda i, j: (0, j)),
            pl.BlockSpec((tm, tn), lambda i, j: (i, j)),
            pl.BlockSpec((tm, tn), lambda i, j: (i, j + nj)),
        ],
        out_specs=pl.BlockSpec((tm, tn), lambda i, j: (i, j)),
        out_shape=jax.ShapeDtypeStruct((t, D_MODEL), BF16),
        compiler_params=_params(("parallel", "arbitrary")),
        name="gated_merge",
    )(ya, yb, w_o_a, w_o_b, gates, gates)


def _out_proj_kernel(a_ref, w_ref, x_ref, o_ref):
    o_ref[...] = x_ref[...] + jnp.dot(a_ref[...], w_ref[...], preferred_element_type=F32)


def out_proj_residual(a, w, x, tm_pref=512, tn_pref=512):
    t, k = a.shape
    n = w.shape[1]
    tm, tn = _pick(t, tm_pref), _pick(n, tn_pref)
    return pl.pallas_call(
        _out_proj_kernel,
        grid=(t // tm, n // tn),
        in_specs=[
            pl.BlockSpec((tm, k), lambda i, j: (i, 0)),
            pl.BlockSpec((k, tn), lambda i, j: (0, j)),
            pl.BlockSpec((tm, tn), lambda i, j: (i, j)),
        ],
        out_specs=pl.BlockSpec((tm, tn), lambda i, j: (i, j)),
        out_shape=jax.ShapeDtypeStruct((t, n), F32),
        compiler_params=_params(("parallel", "arbitrary")),
        name="out_proj_residual",
    )(a, w, x)


def _top_values(s, k):
    rows = []
    for _ in range(k):
        m = jnp.max(s, axis=0, keepdims=True)
        rows.append(m)
        s = jnp.where(s == m, -jnp.inf, s)
    return rows


def _peer_route_kernel(h_ref, g_ref, wq_ref, sk1_ref, sk2_ref,
                       hf_ref, th1_ref, r1_ref, s2_ref, e2_ref, qt_ref):
    hf = _rms(h_ref[...], g_ref[...]).astype(BF16)
    hf_ref[...] = hf
    qt_ref[...] = lax.dot_general(wq_ref[...], hf, _NT, preferred_element_type=F32).astype(BF16)
    half = PEER_KEY_DIM // 2
    tc = hf.shape[0]
    row16 = lax.broadcasted_iota(jnp.int32, (PEER_TOPK, tc), 0)

    def head(h, carry):
        base = pl.multiple_of(h * PEER_KEY_DIM, PEER_KEY_DIM)
        q1 = qt_ref[pl.ds(base, half), :]
        q2 = qt_ref[pl.ds(base + half, half), :]
        s1 = jnp.dot(sk1_ref[...], q1, preferred_element_type=F32)
        s2 = jnp.dot(sk2_ref[...], q2, preferred_element_type=F32)
        v1 = _top_values(s1, PEER_TOPK + 1)
        v2_rows = _top_values(s2, PEER_TOPK + 1)
        v2 = jnp.zeros((PEER_TOPK, tc), F32)
        for r, row in enumerate(v2_rows[:PEER_TOPK]):
            v2 = jnp.where(row16 == r, row, v2)
        cand = jnp.concatenate([v1[a] + v2 for a in range(PEER_TOPK)], axis=0)
        rem = cand
        tau = jnp.zeros((1, tc), F32)
        cnt = jnp.zeros((1, tc), F32)
        for _ in range(PEER_TOPK):
            m = jnp.max(rem, axis=0, keepdims=True)
            hit = rem == m
            tau = jnp.where(cnt < PEER_TOPK, m, tau)
            cnt = cnt + jnp.sum(jnp.where(hit, 1.0, 0.0), axis=0, keepdims=True)
            rem = jnp.where(hit, -jnp.inf, rem)
        below = jnp.max(jnp.where(cand < tau, cand, -jnp.inf), axis=0, keepdims=True)
        below = jnp.maximum(below, jnp.maximum(v1[PEER_TOPK] + v2_rows[0], v1[0] + v2_rows[PEER_TOPK]))
        tau = 0.5 * (tau + below)
        m1, m2 = v1[0], v2_rows[0]
        z = jnp.zeros((1, tc), F32)
        for a in range(PEER_TOPK):
            sel = v2 >= tau - v1[a]
            z = z + jnp.sum(jnp.where(sel, jnp.exp(v1[a] + v2 - (m1 + m2)), 0.0), axis=0, keepdims=True)
        th1_ref[h] = tau - s1
        r1_ref[h] = jnp.exp(s1 - m1) / z
        s2_ref[h] = s2
        e2_ref[h] = jnp.exp(s2 - m2)
        return carry

    lax.fori_loop(0, PEER_HEADS, head, 0)


def peer_route(h1, norm_ffn, wq_t, sk1, sk2, tc_pref=256):
    t = h1.shape[0]
    tc = _pick(t, tc_pref)
    full = lambda a: pl.BlockSpec(a.shape, lambda i: (0,) * a.ndim)
    g = norm_ffn.reshape(1, -1).astype(F32)
    keyed = pl.BlockSpec((PEER_HEADS, N_KEYS, tc), lambda i: (0, 0, i))
    keyed_shape = jax.ShapeDtypeStruct((PEER_HEADS, N_KEYS, t), F32)
    return pl.pallas_call(
        _peer_route_kernel,
        grid=(t // tc,),
        in_specs=[pl.BlockSpec((tc, D_MODEL), lambda i: (i, 0)), full(g), full(wq_t), full(sk1), full(sk2)],
        out_specs=[pl.BlockSpec((tc, D_MODEL), lambda i: (i, 0)), keyed, keyed, keyed, keyed],
        out_shape=[jax.ShapeDtypeStruct((t, D_MODEL), BF16), keyed_shape, keyed_shape, keyed_shape,
                   keyed_shape],
        scratch_shapes=[pltpu.VMEM((PEER_HEADS * PEER_KEY_DIM, tc), BF16)],
        compiler_params=_params(("parallel",)),
        name="peer_route",
    )(h1, g, wq_t, sk1, sk2)


def _peer_expert_kernel(hf_ref, dn_ref, upt_ref, th1_ref, r1_ref, s2_ref, e2_ref,
                        h1_ref, gf_ref, o_ref, acc_ref, g_ref, wt_ref, *, ib, ne):
    e = pl.program_id(1)

    @pl.when(e == 0)
    def _():
        acc_ref[...] = jnp.zeros_like(acc_ref)

    tc = hf_ref.shape[0]
    for il in range(ib):
        es = slice(il * N_KEYS, (il + 1) * N_KEYS)
        for lc in range(tc // LANES):
            ls = slice(lc * LANES, (lc + 1) * LANES)
            g = jnp.zeros((N_KEYS, LANES), F32)
            for h in range(PEER_HEADS):
                sel = s2_ref[h, :, ls] >= th1_ref[h, il:il + 1, ls]
                g = g + jnp.where(sel, e2_ref[h, :, ls], 0.0) * r1_ref[h, il:il + 1, ls]
            g_ref[es, ls] = g
    at = lax.dot_general(dn_ref[...], hf_ref[...], _NT, preferred_element_type=F32)
    wt_ref[...] = (0.5 * at * (1.0 + lax.erf(at * (2.0 ** -0.5))) * g_ref[...]).astype(BF16)
    acc_ref[...] += jnp.dot(upt_ref[...], wt_ref[...], preferred_element_type=F32)

    @pl.when(e == ne - 1)
    def _():
        o_ref[...] = _rms(h1_ref[...] + acc_ref[...].T, gf_ref[...])


def peer_experts(hf, down, up_t, th1, r1, s2, e2, h1, g_final, tc_pref=512, eb=1024):
    t = hf.shape[0]
    tc = _pick(t, tc_pref)
    ib = eb // N_KEYS
    ne = N_EXPERTS // eb
    gf = g_final.reshape(1, -1).astype(F32)
    blk_i = pl.BlockSpec((PEER_HEADS, ib, tc), lambda c, e: (0, e, c))
    blk_j = pl.BlockSpec((PEER_HEADS, N_KEYS, tc), lambda c, e: (0, 0, c))
    once = pl.Buffered(1)
    return pl.pallas_call(
        functools.partial(_peer_expert_kernel, ib=ib, ne=ne),
        grid=(t // tc, ne),
        in_specs=[
            pl.BlockSpec((tc, D_MODEL), lambda c, e: (c, 0), pipeline_mode=once),
            pl.BlockSpec((eb, D_MODEL), lambda c, e: (e, 0)),
            pl.BlockSpec((D_MODEL, eb), lambda c, e: (0, e)),
            blk_i, blk_i, blk_j, blk_j,
            pl.BlockSpec((tc, D_MODEL), lambda c, e: (c, 0), pipeline_mode=once),
            pl.BlockSpec((1, D_MODEL), lambda c, e: (0, 0)),
        ],
        out_specs=pl.BlockSpec((tc, D_MODEL), lambda c, e: (c, 0)),
        out_shape=jax.ShapeDtypeStruct((t, D_MODEL), F32),
        scratch_shapes=[pltpu.VMEM((D_MODEL, tc), F32), pltpu.VMEM((eb, tc), F32), pltpu.VMEM((eb, tc), BF16)],
        compiler_params=_params(("parallel", "arbitrary")),
        name="peer_experts",
    )(hf, down, up_t, th1, r1, s2, e2, h1, gf)


def _rel_bucket(rel):
    nb = REL_BUCKETS // 2
    max_exact = nb // 2
    ret = jnp.where(rel > 0, nb, 0)
    n = jnp.abs(rel)
    nf = jnp.maximum(n, 1).astype(F32)
    large = max_exact + (jnp.log(nf / max_exact) / math.log(REL_MAX_DIST / max_exact)
                         * (nb - max_exact)).astype(jnp.int32)
    large = jnp.minimum(large, nb - 1)
    return ret + jnp.where(n < max_exact, n, large)


def _window_bias(rel_table, sink):
    i = jnp.arange(BLOCK)
    c = jnp.arange(3 * BLOCK)
    rel_w = c[None, :] - BLOCK - i[:, None]
    bw = rel_table[_rel_bucket(rel_w)].astype(F32).transpose(2, 0, 1)
    bw = jnp.where((jnp.abs(rel_w) <= WINDOW)[None], bw, NEG)
    bw = bw.reshape(A_KV_HEADS, A_GROUP * BLOCK, 3 * BLOCK)
    q_pos = N_META + jnp.arange(2)[:, None] * BLOCK + i[None, :]
    rel_m = jnp.arange(N_META)[None, None, :] - q_pos[:, :, None]
    bm = rel_table[_rel_bucket(rel_m)].astype(F32).transpose(0, 3, 1, 2)
    sink_col = jnp.broadcast_to(sink.astype(F32)[None, :, None, None], (2, A_HEADS, BLOCK, 1))
    pad = jnp.full((2, A_HEADS, BLOCK, META_PAD - N_META - 1), NEG, F32)
    bm = jnp.concatenate([bm, sink_col, pad], axis=-1)
    return bw, bm.reshape(2, A_KV_HEADS, A_GROUP * BLOCK, META_PAD)


def _rope_tables(length):
    half = QK_ROPE // 2
    freqs = ROPE_THETA ** (-jnp.arange(half, dtype=F32) / half)
    ang = jnp.arange(length).astype(F32)[:, None] * freqs[None, :]
    c, s = jnp.cos(ang), jnp.sin(ang)
    z = jnp.zeros((length, LANES - QK_ROPE), F32)
    zh = jnp.zeros((length, half), F32)
    return (jnp.concatenate([c, c, z], axis=1),
            jnp.concatenate([-s, zh, z], axis=1),
            jnp.concatenate([zh, s, z], axis=1))


def _pad_rows(a, rows):
    return jnp.pad(a, ((0, rows - a.shape[0]), (0, 0)))


def kernel(x_prompt, x_sample, meta_tokens, rel_table, norm_mix, w_in, g_cq, w_uq, g_ckv, w_ukv, attn_sink,
           w_o_a, w_o_b, w_out, norm_ffn, w_query, sub_keys1, sub_keys2, expert_down, expert_up, g_final):
    assert w_in.shape[0] == 1, "single-layer block"
    wi = w_in[0]
    c1 = QKV_COLS
    c2 = c1 + Q_LORA + KV_LORA + QK_ROPE
    w_qkv = wi[:, :c1].astype(BF16)
    w_lat = jnp.pad(wi[:, c1:c2], ((0, 0), (0, LAT_COLS - (c2 - c1)))).astype(BF16)
    w_gate = wi[:, c2:].astype(BF16)
    wq = jnp.pad(w_uq[0].reshape(Q_LORA, B_HEADS, QK_NOPE + QK_ROPE),
                 ((0, 0), (0, 0), (0, QK_PAD - QK_NOPE - QK_ROPE))).reshape(Q_LORA, B_HEADS * QK_PAD).astype(BF16)
    wkv = w_ukv[0].reshape(KV_LORA, B_HEADS, QK_NOPE + V_HEAD)
    wk = wkv[:, :, :QK_NOPE].reshape(KV_LORA, B_HEADS * QK_NOPE).astype(BF16)
    wv = wkv[:, :, QK_NOPE:].reshape(KV_LORA, B_HEADS * V_HEAD).astype(BF16)
    woa, wob, wout = w_o_a[0].astype(BF16), w_o_b[0].astype(BF16), w_out[0].astype(BF16)
    wq_t = w_query[0].T.astype(BF16)
    sk1, sk2 = sub_keys1[0].astype(BF16), sub_keys2[0].astype(BF16)
    down, up_t = expert_down[0].astype(BF16), expert_up[0].astype(BF16).T
    bias_w, bias_m = _window_bias(rel_table, attn_sink[0])

    zm_qkv = norm_matmul(meta_tokens, norm_mix[0], w_qkv, BF16)
    zm_lat = norm_matmul(meta_tokens, norm_mix[0], w_lat, F32)
    _, km, vm = mla_proj(zm_lat, g_cq[0], g_ckv[0], wq, wk, wv, _rope_tables(N_META), N_META)
    zm_qkv, km, vm = _pad_rows(zm_qkv, META_PAD), _pad_rows(km, META_PAD), _pad_rows(vm, META_PAD)

    def encode(x):
        bsz, seq, _ = x.shape
        xt = x.reshape(bsz * seq, D_MODEL)
        tabs = tuple(t[N_META:] for t in _rope_tables(N_META + seq))
        z_qkv = norm_matmul(xt, norm_mix[0], w_qkv, BF16)
        z_lat = norm_matmul(xt, norm_mix[0], w_lat, F32)
        gates = norm_matmul(xt, norm_mix[0], w_gate, BF16)
        q, k, v = mla_proj(z_lat, g_cq[0], g_ckv[0], wq, wk, wv, tabs, seq)
        y_a = window_attention(z_qkv, zm_qkv, bias_w, bias_m, bsz, seq)
        y_b = mla_attention(q, k, v, km, vm, bsz, seq)
        merged = gated_merge(y_a, y_b, woa, wob, gates)
        h1 = out_proj_residual(merged, wout, xt)
        hf, th1, r1, s2, e2 = peer_route(h1, norm_ffn[0], wq_t, sk1, sk2)
        y = peer_experts(hf, down, up_t, th1, r1, s2, e2, h1, g_final)
        return y.reshape(bsz, seq, D_MODEL)

    return encode(x_prompt), encode(x_sample)
```

```python
import functools
import math

import jax
import jax.numpy as jnp
from jax import lax
from jax.experimental import pallas as pl
from jax.experimental.pallas import tpu as pltpu

F32 = jnp.float32
BF16 = jnp.bfloat16

D_MODEL = 2048
N_META = 16
BLOCK = 128
WINDOW = 128
A_HEADS = 16
A_KV_HEADS = 4
A_HEAD_DIM = 128
A_GROUP = A_HEADS // A_KV_HEADS
B_HEADS = 16
Q_LORA = 512
KV_LORA = 256
QK_NOPE = 128
QK_ROPE = 64
V_HEAD = 128
ROPE_THETA = 10000.0
REL_BUCKETS = 32
REL_MAX_DIST = 128
PEER_HEADS = 8
N_KEYS = 128
N_EXPERTS = N_KEYS * N_KEYS
PEER_KEY_DIM = 128
PEER_TOPK = 16
EPS = 1e-6
NEG = -1e30

A_Q_COLS = A_HEADS * A_HEAD_DIM
A_KV_COLS = A_KV_HEADS * A_HEAD_DIM
QKV_COLS = A_Q_COLS + 2 * A_KV_COLS
LAT_COLS = 896
QK_PAD = 256
LANES = 128
SUBLANES = 8
META_PAD = 128
VMEM_LIMIT = 56 * 1024 * 1024

_NT = (((1,), (1,)), ((), ()))
_TN = (((0,), (0,)), ((), ()))


def _params(sem):
    return pltpu.CompilerParams(dimension_semantics=sem, vmem_limit_bytes=VMEM_LIMIT)


def _pick(n, pref):
    if n <= pref:
        return n
    t = pref
    while n % t:
        t //= 2
    return t


def _rms(xf, g):
    ms = jnp.mean(xf * xf, axis=-1, keepdims=True)
    return xf * lax.rsqrt(ms + EPS) * g


def _norm_matmul_kernel(x_ref, g_ref, w_ref, o_ref, xn_ref):
    @pl.when(pl.program_id(1) == 0)
    def _():
        xn_ref[...] = _rms(x_ref[...].astype(F32), g_ref[...]).astype(BF16)

    o_ref[...] = jnp.dot(xn_ref[...], w_ref[...], preferred_element_type=F32).astype(o_ref.dtype)


def norm_matmul(x, g, w, out_dtype, tm_pref=1024, tn_pref=1024):
    m, k = x.shape
    n = w.shape[1]
    tm, tn = _pick(m, tm_pref), _pick(n, tn_pref)
    return pl.pallas_call(
        _norm_matmul_kernel,
        grid=(m // tm, n // tn),
        in_specs=[
            pl.BlockSpec((tm, k), lambda i, j: (i, 0)),
            pl.BlockSpec((1, k), lambda i, j: (0, 0)),
            pl.BlockSpec((k, tn), lambda i, j: (0, j)),
        ],
        out_specs=pl.BlockSpec((tm, tn), lambda i, j: (i, j)),
        out_shape=jax.ShapeDtypeStruct((m, n), out_dtype),
        scratch_shapes=[pltpu.VMEM((tm, k), BF16)],
        compiler_params=_params(("parallel", "arbitrary")),
        name="norm_matmul",
    )(x, g.reshape(1, k).astype(F32), w)


def _rope128(xg, cos, sin_lo, sin_hi):
    return xg * cos + pltpu.roll(xg, 96, 1) * sin_lo + pltpu.roll(xg, 32, 1) * sin_hi


def _mla_proj_kernel(z_ref, gq_ref, gkv_ref, wq_ref, wk_ref, wv_ref, cos_ref, slo_ref, shi_ref,
                     q_ref, k_ref, v_ref, *, q_scale):
    z = z_ref[...]
    nq = _rms(z[:, :Q_LORA], gq_ref[...]).astype(BF16)
    nkv = _rms(z[:, Q_LORA:Q_LORA + KV_LORA], gkv_ref[...]).astype(BF16)
    cos, slo, shi = cos_ref[...], slo_ref[...], shi_ref[...]
    kr = _rope128(z[:, Q_LORA + KV_LORA:], cos, slo, shi).astype(BF16)
    for h in range(B_HEADS):
        qh = jnp.dot(nq, wq_ref[:, h * QK_PAD:(h + 1) * QK_PAD], preferred_element_type=F32)
        q_ref[:, h * QK_PAD:h * QK_PAD + LANES] = (qh[:, :LANES] * q_scale).astype(BF16)
        q_ref[:, h * QK_PAD + LANES:(h + 1) * QK_PAD] = (
            _rope128(qh[:, LANES:], cos, slo, shi) * q_scale).astype(BF16)
        kh = jnp.dot(nkv, wk_ref[:, h * QK_NOPE:(h + 1) * QK_NOPE], preferred_element_type=F32)
        k_ref[:, h * QK_PAD:h * QK_PAD + LANES] = kh.astype(BF16)
        k_ref[:, h * QK_PAD + LANES:(h + 1) * QK_PAD] = kr
        vh = jnp.dot(nkv, wv_ref[:, h * V_HEAD:(h + 1) * V_HEAD], preferred_element_type=F32)
        v_ref[:, h * V_HEAD:(h + 1) * V_HEAD] = vh.astype(BF16)


def mla_proj(z_lat, g_cq, g_ckv, wq, wk, wv, rope_tabs, seq, tm_pref=256):
    t = z_lat.shape[0]
    tm = _pick(min(t, seq), tm_pref)
    nseq = seq // tm
    q_scale = (QK_NOPE + QK_ROPE) ** -0.5 * math.log2(math.e)
    full = lambda a: pl.BlockSpec(a.shape, lambda i: (0, 0))
    tab = pl.BlockSpec((tm, LANES), lambda i: (i % nseq, 0))
    gq, gkv = g_cq.reshape(1, -1).astype(F32), g_ckv.reshape(1, -1).astype(F32)
    return pl.pallas_call(
        functools.partial(_mla_proj_kernel, q_scale=q_scale),
        grid=(t // tm,),
        in_specs=[pl.BlockSpec((tm, LAT_COLS), lambda i: (i, 0)), full(gq), full(gkv),
                  full(wq), full(wk), full(wv), tab, tab, tab],
        out_specs=[pl.BlockSpec((tm, B_HEADS * QK_PAD), lambda i: (i, 0)),
                   pl.BlockSpec((tm, B_HEADS * QK_PAD), lambda i: (i, 0)),
                   pl.BlockSpec((tm, B_HEADS * V_HEAD), lambda i: (i, 0))],
        out_shape=[jax.ShapeDtypeStruct((t, B_HEADS * QK_PAD), BF16),
                   jax.ShapeDtypeStruct((t, B_HEADS * QK_PAD), BF16),
                   jax.ShapeDtypeStruct((t, B_HEADS * V_HEAD), BF16)],
        compiler_params=_params(("parallel",)),
        name="mla_proj",
    )(z_lat, gq, gkv, wq, wk, wv, *rope_tabs)


def _window_kernel(q_ref, kl_ref, kc_ref, kr_ref, vl_ref, vc_ref, vr_ref, km_ref, vm_ref,
                   bw_ref, bm_ref, o_ref, *, nblk):
    n = pl.program_id(1)
    scale = A_HEAD_DIM ** -0.5
    col = lax.broadcasted_iota(jnp.int32, (1, 3 * BLOCK), 1)
    left_pen = jnp.where(n > 0, 0.0, NEG).astype(F32)
    right_pen = jnp.where(n < nblk - 1, 0.0, NEG).astype(F32)
    edge = jnp.where(col < BLOCK, left_pen, 0.0) + jnp.where(col >= 2 * BLOCK, right_pen, 0.0)
    for kh in range(A_KV_HEADS):
        cs = slice(kh * A_HEAD_DIM, (kh + 1) * A_HEAD_DIM)
        kband = jnp.concatenate([kl_ref[:, cs], kc_ref[:, cs], kr_ref[:, cs]], axis=0)
        vband = jnp.concatenate([vl_ref[:, cs], vc_ref[:, cs], vr_ref[:, cs]], axis=0)
        q4 = jnp.concatenate(
            [q_ref[:, (kh * A_GROUP + g) * A_HEAD_DIM:(kh * A_GROUP + g + 1) * A_HEAD_DIM]
             for g in range(A_GROUP)], axis=0)
        s_w = lax.dot_general(q4, kband, _NT, preferred_element_type=F32) * scale + bw_ref[kh] + edge
        s_m = lax.dot_general(q4, km_ref[:, cs], _NT, preferred_element_type=F32) * scale + bm_ref[0, kh]
        m = jnp.maximum(jnp.max(s_w, axis=-1, keepdims=True), jnp.max(s_m, axis=-1, keepdims=True))
        p_w = jnp.exp(s_w - m)
        p_m = jnp.exp(s_m - m)
        denom = jnp.sum(p_w, axis=-1, keepdims=True) + jnp.sum(p_m, axis=-1, keepdims=True)
        o = (jnp.dot(p_w.astype(BF16), vband, preferred_element_type=F32)
             + jnp.dot(p_m.astype(BF16), vm_ref[:, cs], preferred_element_type=F32)) / denom
        for g in range(A_GROUP):
            h = kh * A_GROUP + g
            o_ref[:, h * A_HEAD_DIM:(h + 1) * A_HEAD_DIM] = o[g * BLOCK:(g + 1) * BLOCK].astype(o_ref.dtype)


def window_attention(z_qkv, zm_qkv, bias_w, bias_m, bsz, seq):
    nblk = seq // BLOCK
    kcol, vcol = A_Q_COLS // A_KV_COLS, A_Q_COLS // A_KV_COLS + 1
    row = lambda b, n: b * nblk + n
    lo = lambda b, n: b * nblk + jnp.maximum(n - 1, 0)
    hi = lambda b, n: b * nblk + jnp.minimum(n + 1, nblk - 1)
    kv = lambda r, c: pl.BlockSpec((BLOCK, A_KV_COLS), lambda b, n: (r(b, n), c))
    return pl.pallas_call(
        functools.partial(_window_kernel, nblk=nblk),
        grid=(bsz, nblk),
        in_specs=[
            pl.BlockSpec((BLOCK, A_Q_COLS), lambda b, n: (row(b, n), 0)),
            kv(lo, kcol), kv(row, kcol), kv(hi, kcol),
            kv(lo, vcol), kv(row, vcol), kv(hi, vcol),
            pl.BlockSpec((META_PAD, A_KV_COLS), lambda b, n: (0, kcol)),
            pl.BlockSpec((META_PAD, A_KV_COLS), lambda b, n: (0, vcol)),
            pl.BlockSpec(bias_w.shape, lambda b, n: (0, 0, 0)),
            pl.BlockSpec((1,) + bias_m.shape[1:], lambda b, n: (jnp.minimum(n, 1), 0, 0, 0)),
        ],
        out_specs=pl.BlockSpec((BLOCK, A_Q_COLS), lambda b, n: (row(b, n), 0)),
        out_shape=jax.ShapeDtypeStruct((bsz * seq, A_Q_COLS), BF16),
        compiler_params=_params(("parallel", "arbitrary")),
        name="window_attention",
    )(z_qkv, z_qkv, z_qkv, z_qkv, z_qkv, z_qkv, z_qkv, zm_qkv, zm_qkv, bias_w, bias_m)


def _mla_kernel(q_ref, k_ref, v_ref, km_ref, vm_ref, o_ref, acc_ref, sa_ref, sb_ref, pa_ref, pb_ref,
                *, tk, npair):
    last = 2 * npair - 1

    def scores(c):
        start = pl.multiple_of(c * tk, tk)
        return lax.dot_general(q_ref[...], k_ref[pl.ds(start, tk), :], _NT, preferred_element_type=F32)

    def weighted(p_ref, c):
        start = pl.multiple_of(c * tk, tk)
        return jnp.dot(p_ref[...], v_ref[pl.ds(start, tk), :], preferred_element_type=F32)

    def softmax(s_ref, p_ref, m, l):
        s = s_ref[...]
        m_new = jnp.maximum(m, jnp.max(s, axis=-1, keepdims=True))
        alpha = jnp.exp2(m - m_new)
        p = jnp.exp2(s - m_new)
        p_ref[...] = p.astype(BF16)
        return m_new, alpha * l + jnp.sum(p, axis=-1, keepdims=True), alpha

    s0 = lax.dot_general(q_ref[...], km_ref[...], _NT, preferred_element_type=F32)
    mcol = lax.broadcasted_iota(jnp.int32, (1, META_PAD), 1)
    s0 = jnp.where(mcol < N_META, s0, NEG)
    m0 = jnp.max(s0, axis=-1, keepdims=True)
    p0 = jnp.exp2(s0 - m0)
    l0 = jnp.sum(p0, axis=-1, keepdims=True)
    acc_ref[...] = jnp.dot(p0.astype(BF16), vm_ref[...], preferred_element_type=F32)
    sa_ref[...] = scores(0)
    pb_ref[...] = jnp.zeros_like(pb_ref)

    def body(j, carry):
        m, l, alpha_b = carry
        c0 = 2 * j
        sb_ref[...] = scores(c0 + 1)
        acc_ref[...] = alpha_b * acc_ref[...] + weighted(pb_ref, jnp.maximum(c0 - 1, 0))
        m, l, alpha_a = softmax(sa_ref, pa_ref, m, l)
        sa_ref[...] = scores(jnp.minimum(c0 + 2, last))
        acc_ref[...] = alpha_a * acc_ref[...] + weighted(pa_ref, c0)
        m, l, alpha_b = softmax(sb_ref, pb_ref, m, l)
        return m, l, alpha_b

    _, l, alpha_b = lax.fori_loop(0, npair, body, (m0, l0, jnp.ones_like(m0)))
    acc = alpha_b * acc_ref[...] + weighted(pb_ref, last)
    o_ref[...] = (acc / l).astype(o_ref.dtype)


def mla_attention(q, k, v, km, vm, bsz, seq, tq_pref=512, tk_pref=1024):
    tq, tk = _pick(seq, tq_pref), _pick(seq // 2, tk_pref)
    nq = seq // tq
    return pl.pallas_call(
        functools.partial(_mla_kernel, tk=tk, npair=seq // (2 * tk)),
        grid=(bsz, B_HEADS, nq),
        in_specs=[
            pl.BlockSpec((tq, QK_PAD), lambda b, h, i: (b * nq + i, h)),
            pl.BlockSpec((seq, QK_PAD), lambda b, h, i: (b, h)),
            pl.BlockSpec((seq, V_HEAD), lambda b, h, i: (b, h)),
            pl.BlockSpec((META_PAD, QK_PAD), lambda b, h, i: (0, h)),
            pl.BlockSpec((META_PAD, V_HEAD), lambda b, h, i: (0, h)),
        ],
        out_specs=pl.BlockSpec((tq, V_HEAD), lambda b, h, i: (b * nq + i, h)),
        out_shape=jax.ShapeDtypeStruct((bsz * seq, B_HEADS * V_HEAD), BF16),
        scratch_shapes=[pltpu.VMEM((tq, V_HEAD), F32),
                        pltpu.VMEM((tq, tk), F32), pltpu.VMEM((tq, tk), F32),
                        pltpu.VMEM((tq, tk), BF16), pltpu.VMEM((tq, tk), BF16)],
        compiler_params=_params(("parallel", "parallel", "arbitrary")),
        name="mla_attention",
    )(q, k, v, km, vm)


def _merge_kernel(ya_ref, yb_ref, wa_ref, wb_ref, ga_ref, gb_ref, o_ref):
    pa = jnp.dot(ya_ref[...], wa_ref[...], preferred_element_type=F32)
    pb = jnp.dot(yb_ref[...], wb_ref[...], preferred_element_type=F32)
    o_ref[...] = (jax.nn.sigmoid(ga_ref[...].astype(F32)) * pa
                  + jax.nn.sigmoid(gb_ref[...].astype(F32)) * pb).astype(o_ref.dtype)


def gated_merge(ya, yb, w_o_a, w_o_b, gates, tm_pref=1024, tn_pref=1024):
    t = ya.shape[0]
    tm, tn = _pick(t, tm_pref), _pick(D_MODEL, tn_pref)
    nj = D_MODEL // tn
    return pl.pallas_call(
        _merge_kernel,
        grid=(t // tm, nj),
        in_specs=[
            pl.BlockSpec((tm, A_Q_COLS), lambda i, j: (i, 0)),
            pl.BlockSpec((tm, B_HEADS * V_HEAD), lambda i, j: (i, 0)),
            pl.BlockSpec((A_Q_COLS, tn), lambda i, j: (0, j)),
            pl.BlockSpec((B_HEADS * V_HEAD, tn), lambda i, j: (0, j)),
            pl.BlockSpec((tm, tn), lambda i, j: (i, j)),
            pl.BlockSpec((tm, tn), lambda i, j: (i, j + nj)),
        ],
        out_specs=pl.BlockSpec((tm, tn), lambda i, j: (i, j)),
        out_shape=jax.ShapeDtypeStruct((t, D_MODEL), BF16),
        compiler_params=_params(("parallel", "arbitrary")),
        name="gated_merge",
    )(ya, yb, w_o_a, w_o_b, gates, gates)


def _out_proj_kernel(a_ref, w_ref, x_ref, o_ref):
    o_ref[...] = x_ref[...] + jnp.dot(a_ref[...], w_ref[...], preferred_element_type=F32)


def out_proj_residual(a, w, x, tm_pref=1024, tn_pref=1024):
    t, k = a.shape
    n = w.shape[1]
    tm, tn = _pick(t, tm_pref), _pick(n, tn_pref)
    return pl.pallas_call(
        _out_proj_kernel,
        grid=(t // tm, n // tn),
        in_specs=[
            pl.BlockSpec((tm, k), lambda i, j: (i, 0)),
            pl.BlockSpec((k, tn), lambda i, j: (0, j)),
            pl.BlockSpec((tm, tn), lambda i, j: (i, j)),
        ],
        out_specs=pl.BlockSpec((tm, tn), lambda i, j: (i, j)),
        out_shape=jax.ShapeDtypeStruct((t, n), F32),
        compiler_params=_params(("parallel", "arbitrary")),
        name="out_proj_residual",
    )(a, w, x)


def _top_values(s, k):
    rows = []
    for _ in range(k):
        m = jnp.max(s, axis=0, keepdims=True)
        rows.append(m)
        s = jnp.where(s == m, -jnp.inf, s)
    return rows


def _peer_route_kernel(h_ref, g_ref, wq_ref, sk1_ref, sk2_ref,
                       hf_ref, th1_ref, r1_ref, s2_ref, e2_ref, qt_ref):
    hf = _rms(h_ref[...], g_ref[...]).astype(BF16)
    hf_ref[...] = hf
    qt_ref[...] = lax.dot_general(wq_ref[...], hf, _NT, preferred_element_type=F32).astype(BF16)
    half = PEER_KEY_DIM // 2
    tc = hf.shape[0]
    row16 = lax.broadcasted_iota(jnp.int32, (PEER_TOPK, tc), 0)

    def head(h, carry):
        base = pl.multiple_of(h * PEER_KEY_DIM, PEER_KEY_DIM)
        q1 = qt_ref[pl.ds(base, half), :]
        q2 = qt_ref[pl.ds(base + half, half), :]
        s1 = jnp.dot(sk1_ref[...], q1, preferred_element_type=F32)
        s2 = jnp.dot(sk2_ref[...], q2, preferred_element_type=F32)
        v1 = _top_values(s1, PEER_TOPK + 1)
        v2_rows = _top_values(s2, PEER_TOPK + 1)
        v2 = jnp.zeros((PEER_TOPK, tc), F32)
        for r, row in enumerate(v2_rows[:PEER_TOPK]):
            v2 = jnp.where(row16 == r, row, v2)
        cand = jnp.concatenate([v1[a] + v2 for a in range(PEER_TOPK)], axis=0)
        rem = cand
        tau = jnp.zeros((1, tc), F32)
        cnt = jnp.zeros((1, tc), F32)
        for _ in range(PEER_TOPK):
            m = jnp.max(rem, axis=0, keepdims=True)
            hit = rem == m
            tau = jnp.where(cnt < PEER_TOPK, m, tau)
            cnt = cnt + jnp.sum(jnp.where(hit, 1.0, 0.0), axis=0, keepdims=True)
            rem = jnp.where(hit, -jnp.inf, rem)
        below = jnp.max(jnp.where(cand < tau, cand, -jnp.inf), axis=0, keepdims=True)
        below = jnp.maximum(below, jnp.maximum(v1[PEER_TOPK] + v2_rows[0], v1[0] + v2_rows[PEER_TOPK]))
        tau = 0.5 * (tau + below)
        m1, m2 = v1[0], v2_rows[0]
        z = jnp.zeros((1, tc), F32)
        for a in range(PEER_TOPK):
            sel = v2 >= tau - v1[a]
            z = z + jnp.sum(jnp.where(sel, jnp.exp(v1[a] + v2 - (m1 + m2)), 0.0), axis=0, keepdims=True)
        th1_ref[h] = tau - s1
        r1_ref[h] = jnp.exp(s1 - m1) / z
        s2_ref[h] = s2
        e2_ref[h] = jnp.exp(s2 - m2)
        return carry

    lax.fori_loop(0, PEER_HEADS, head, 0)


def peer_route(h1, norm_ffn, wq_t, sk1, sk2, tc_pref=256):
    t = h1.shape[0]
    tc = _pick(t, tc_pref)
    full = lambda a: pl.BlockSpec(a.shape, lambda i: (0,) * a.ndim)
    g = norm_ffn.reshape(1, -1).astype(F32)
    keyed = pl.BlockSpec((PEER_HEADS, N_KEYS, tc), lambda i: (0, 0, i))
    keyed_shape = jax.ShapeDtypeStruct((PEER_HEADS, N_KEYS, t), F32)
    return pl.pallas_call(
        _peer_route_kernel,
        grid=(t // tc,),
        in_specs=[pl.BlockSpec((tc, D_MODEL), lambda i: (i, 0)), full(g), full(wq_t), full(sk1), full(sk2)],
        out_specs=[pl.BlockSpec((tc, D_MODEL), lambda i: (i, 0)), keyed, keyed, keyed, keyed],
        out_shape=[jax.ShapeDtypeStruct((t, D_MODEL), BF16), keyed_shape, keyed_shape, keyed_shape,
                   keyed_shape],
        scratch_shapes=[pltpu.VMEM((PEER_HEADS * PEER_KEY_DIM, tc), BF16)],
        compiler_params=_params(("parallel",)),
        name="peer_route",
    )(h1, g, wq_t, sk1, sk2)


GATE_ROWS = 32
GATE_PIECES = 8


def _peer_expert_kernel(hf_ref, dn_ref, upt_ref, th1_ref, r1_ref, th1n_ref, r1n_ref, s2_ref, e2_ref,
                        h1_ref, gf_ref, o_ref, acc_ref, ga_ref, gb_ref, wt_ref, bth_ref, br_ref,
                        *, ib, ne):
    e = pl.program_id(1)
    tc = hf_ref.shape[0]
    eb = dn_ref.shape[0]
    strip = GATE_ROWS // SUBLANES

    def gate_rows(th_ref, r_ref, g_ref, il):
        for h in range(PEER_HEADS):
            bth_ref[h, il] = jnp.broadcast_to(th_ref[h, il:il + 1, :], (SUBLANES, tc))
            br_ref[h, il] = jnp.broadcast_to(r_ref[h, il:il + 1, :], (SUBLANES, tc))
        for jr in range(N_KEYS // GATE_ROWS):
            js = slice(jr * strip, (jr + 1) * strip)
            g = jnp.zeros((strip, SUBLANES, tc), F32)
            for h in range(PEER_HEADS):
                sel = s2_ref[h, js] >= bth_ref[h, il][None]
                g = g + jnp.where(sel, e2_ref[h, js], 0.0) * br_ref[h, il][None]
            base = il * (N_KEYS // SUBLANES) + jr * strip
            g_ref[base:base + strip] = g

    @pl.when(e == 0)
    def _():
        acc_ref[...] = jnp.zeros_like(acc_ref)
        for il in range(ib):
            gate_rows(th1_ref, r1_ref, ga_ref, il)

    def block(g_cur_ref, g_next_ref):
        at = lax.dot_general(dn_ref[...], hf_ref[...], _NT, preferred_element_type=F32)
        wt_ref[...] = (0.5 * at * (1.0 + lax.erf(at * (2.0 ** -0.5)))
                       * g_cur_ref[...].reshape(eb, tc)).astype(BF16)
        rows, per = D_MODEL // GATE_PIECES, ib // GATE_PIECES
        for p in range(GATE_PIECES):
            rs = slice(p * rows, (p + 1) * rows)
            acc_ref[rs, :] += jnp.dot(upt_ref[rs, :], wt_ref[...], preferred_element_type=F32)
            for il in range(p * per, (p + 1) * per):
                gate_rows(th1n_ref, r1n_ref, g_next_ref, il)

    @pl.when(e % 2 == 0)
    def _():
        block(ga_ref, gb_ref)

    @pl.when(e % 2 == 1)
    def _():
        block(gb_ref, ga_ref)

    @pl.when(e == ne - 1)
    def _():
        o_ref[...] = _rms(h1_ref[...] + acc_ref[...].T, gf_ref[...])


def peer_experts(hf, down, up_t, th1, r1, s2, e2, h1, g_final, tc_pref=512, eb=1024):
    t = hf.shape[0]
    tc = _pick(t, tc_pref)
    ib = eb // N_KEYS
    ne = N_EXPERTS // eb
    assert ne % 2 == 0 and ib % GATE_PIECES == 0
    gf = g_final.reshape(1, -1).astype(F32)
    blk_i = pl.BlockSpec((PEER_HEADS, ib, tc), lambda c, e: (0, e, c))
    blk_n = pl.BlockSpec((PEER_HEADS, ib, tc), lambda c, e: (0, jnp.minimum(e + 1, ne - 1), c))
    s2, e2 = (a.reshape(PEER_HEADS, N_KEYS // SUBLANES, SUBLANES, t) for a in (s2, e2))
    blk_j = pl.BlockSpec((PEER_HEADS, N_KEYS // SUBLANES, SUBLANES, tc), lambda c, e: (0, 0, 0, c))
    once = pl.Buffered(1)
    return pl.pallas_call(
        functools.partial(_peer_expert_kernel, ib=ib, ne=ne),
        grid=(t // tc, ne),
        in_specs=[
            pl.BlockSpec((tc, D_MODEL), lambda c, e: (c, 0), pipeline_mode=once),
            pl.BlockSpec((eb, D_MODEL), lambda c, e: (e, 0)),
            pl.BlockSpec((D_MODEL, eb), lambda c, e: (0, e)),
            blk_i, blk_i, blk_n, blk_n, blk_j, blk_j,
            pl.BlockSpec((tc, D_MODEL), lambda c, e: (c, 0), pipeline_mode=once),
            pl.BlockSpec((1, D_MODEL), lambda c, e: (0, 0)),
        ],
        out_specs=pl.BlockSpec((tc, D_MODEL), lambda c, e: (c, 0)),
        out_shape=jax.ShapeDtypeStruct((t, D_MODEL), F32),
        scratch_shapes=[pltpu.VMEM((D_MODEL, tc), F32),
                        pltpu.VMEM((eb // SUBLANES, SUBLANES, tc), F32),
                        pltpu.VMEM((eb // SUBLANES, SUBLANES, tc), F32),
                        pltpu.VMEM((eb, tc), BF16),
                        pltpu.VMEM((PEER_HEADS, ib, SUBLANES, tc), F32),
                        pltpu.VMEM((PEER_HEADS, ib, SUBLANES, tc), F32)],
        compiler_params=_params(("parallel", "arbitrary")),
        name="peer_experts",
    )(hf, down, up_t, th1, r1, th1, r1, s2, e2, h1, gf)


def _rel_bucket(rel):
    nb = REL_BUCKETS // 2
    max_exact = nb // 2
    ret = jnp.where(rel > 0, nb, 0)
    n = jnp.abs(rel)
    nf = jnp.maximum(n, 1).astype(F32)
    large = max_exact + (jnp.log(nf / max_exact) / math.log(REL_MAX_DIST / max_exact)
                         * (nb - max_exact)).astype(jnp.int32)
    large = jnp.minimum(large, nb - 1)
    return ret + jnp.where(n < max_exact, n, large)


def _window_bias(rel_table, sink):
    i = jnp.arange(BLOCK)
    c = jnp.arange(3 * BLOCK)
    rel_w = c[None, :] - BLOCK - i[:, None]
    bw = rel_table[_rel_bucket(rel_w)].astype(F32).transpose(2, 0, 1)
    bw = jnp.where((jnp.abs(rel_w) <= WINDOW)[None], bw, NEG)
    bw = bw.reshape(A_KV_HEADS, A_GROUP * BLOCK, 3 * BLOCK)
    q_pos = N_META + jnp.arange(2)[:, None] * BLOCK + i[None, :]
    rel_m = jnp.arange(N_META)[None, None, :] - q_pos[:, :, None]
    bm = rel_table[_rel_bucket(rel_m)].astype(F32).transpose(0, 3, 1, 2)
    sink_col = jnp.broadcast_to(sink.astype(F32)[None, :, None, None], (2, A_HEADS, BLOCK, 1))
    pad = jnp.full((2, A_HEADS, BLOCK, META_PAD - N_META - 1), NEG, F32)
    bm = jnp.concatenate([bm, sink_col, pad], axis=-1)
    return bw, bm.reshape(2, A_KV_HEADS, A_GROUP * BLOCK, META_PAD)


def _rope_tables(length):
    half = QK_ROPE // 2
    freqs = ROPE_THETA ** (-jnp.arange(half, dtype=F32) / half)
    ang = jnp.arange(length).astype(F32)[:, None] * freqs[None, :]
    c, s = jnp.cos(ang), jnp.sin(ang)
    z = jnp.zeros((length, LANES - QK_ROPE), F32)
    zh = jnp.zeros((length, half), F32)
    return (jnp.concatenate([c, c, z], axis=1),
            jnp.concatenate([-s, zh, z], axis=1),
            jnp.concatenate([zh, s, z], axis=1))


def _pad_rows(a, rows):
    return jnp.pad(a, ((0, rows - a.shape[0]), (0, 0)))


def kernel(x_prompt, x_sample, meta_tokens, rel_table, norm_mix, w_in, g_cq, w_uq, g_ckv, w_ukv, attn_sink,
           w_o_a, w_o_b, w_out, norm_ffn, w_query, sub_keys1, sub_keys2, expert_down, expert_up, g_final):
    assert w_in.shape[0] == 1, "single-layer block"
    wi = w_in[0]
    c1 = QKV_COLS
    c2 = c1 + Q_LORA + KV_LORA + QK_ROPE
    w_qkv = wi[:, :c1].astype(BF16)
    w_lat = jnp.pad(wi[:, c1:c2], ((0, 0), (0, LAT_COLS - (c2 - c1)))).astype(BF16)
    w_gate = wi[:, c2:].astype(BF16)
    wq = jnp.pad(w_uq[0].reshape(Q_LORA, B_HEADS, QK_NOPE + QK_ROPE),
                 ((0, 0), (0, 0), (0, QK_PAD - QK_NOPE - QK_ROPE))).reshape(Q_LORA, B_HEADS * QK_PAD).astype(BF16)
    wkv = w_ukv[0].reshape(KV_LORA, B_HEADS, QK_NOPE + V_HEAD)
    wk = wkv[:, :, :QK_NOPE].reshape(KV_LORA, B_HEADS * QK_NOPE).astype(BF16)
    wv = wkv[:, :, QK_NOPE:].reshape(KV_LORA, B_HEADS * V_HEAD).astype(BF16)
    woa, wob, wout = w_o_a[0].astype(BF16), w_o_b[0].astype(BF16), w_out[0].astype(BF16)
    wq_t = w_query[0].T.astype(BF16)
    sk1, sk2 = sub_keys1[0].astype(BF16), sub_keys2[0].astype(BF16)
    down, up_t = expert_down[0].astype(BF16), expert_up[0].astype(BF16).T
    bias_w, bias_m = _window_bias(rel_table, attn_sink[0])

    zm_qkv = norm_matmul(meta_tokens, norm_mix[0], w_qkv, BF16)
    zm_lat = norm_matmul(meta_tokens, norm_mix[0], w_lat, F32)
    _, km, vm = mla_proj(zm_lat, g_cq[0], g_ckv[0], wq, wk, wv, _rope_tables(N_META), N_META)
    zm_qkv, km, vm = _pad_rows(zm_qkv, META_PAD), _pad_rows(km, META_PAD), _pad_rows(vm, META_PAD)

    def encode(x):
        bsz, seq, _ = x.shape
        xt = x.reshape(bsz * seq, D_MODEL)
        tabs = tuple(t[N_META:] for t in _rope_tables(N_META + seq))
        z_qkv = norm_matmul(xt, norm_mix[0], w_qkv, BF16)
        z_lat = norm_matmul(xt, norm_mix[0], w_lat, F32)
        gates = norm_matmul(xt, norm_mix[0], w_gate, BF16)
        q, k, v = mla_proj(z_lat, g_cq[0], g_ckv[0], wq, wk, wv, tabs, seq)
        y_a = window_attention(z_qkv, zm_qkv, bias_w, bias_m, bsz, seq)
        y_b = mla_attention(q, k, v, km, vm, bsz, seq)
        merged = gated_merge(y_a, y_b, woa, wob, gates)
        h1 = out_proj_residual(merged, wout, xt)
        hf, th1, r1, s2, e2 = peer_route(h1, norm_ffn[0], wq_t, sk1, sk2)
        y = peer_experts(hf, down, up_t, th1, r1, s2, e2, h1, g_final)
        return y.reshape(bsz, seq, D_MODEL)

    return encode(x_prompt), encode(x_sample)
```

```python
import functools
import math

import jax
import jax.numpy as jnp
from jax import lax
from jax.experimental import pallas as pl
from jax.experimental.pallas import tpu as pltpu

F32 = jnp.float32
BF16 = jnp.bfloat16

D_MODEL = 2048
N_META = 16
BLOCK = 128
WINDOW = 128
A_HEADS = 16
A_KV_HEADS = 4
A_HEAD_DIM = 128
A_GROUP = A_HEADS // A_KV_HEADS
B_HEADS = 16
Q_LORA = 512
KV_LORA = 256
QK_NOPE = 128
QK_ROPE = 64
V_HEAD = 128
ROPE_THETA = 10000.0
REL_BUCKETS = 32
REL_MAX_DIST = 128
PEER_HEADS = 8
N_KEYS = 128
N_EXPERTS = N_KEYS * N_KEYS
PEER_KEY_DIM = 128
PEER_TOPK = 16
EPS = 1e-6
NEG = -1e30

A_Q_COLS = A_HEADS * A_HEAD_DIM
A_KV_COLS = A_KV_HEADS * A_HEAD_DIM
QKV_COLS = A_Q_COLS + 2 * A_KV_COLS
LAT_COLS = 896
QK_PAD = 256
LANES = 128
SUBLANES = 8
META_PAD = 128
VMEM_LIMIT = 56 * 1024 * 1024

_NT = (((1,), (1,)), ((), ()))
_TN = (((0,), (0,)), ((), ()))


def _params(sem):
    return pltpu.CompilerParams(dimension_semantics=sem, vmem_limit_bytes=VMEM_LIMIT)


def _pick(n, pref):
    if n <= pref:
        return n
    t = pref
    while n % t:
        t //= 2
    return t


def _rms(xf, g):
    ms = jnp.mean(xf * xf, axis=-1, keepdims=True)
    return xf * lax.rsqrt(ms + EPS) * g


def _norm_matmul_kernel(x_ref, g_ref, w_ref, o_ref, xn_ref):
    @pl.when(pl.program_id(1) == 0)
    def _():
        xn_ref[...] = _rms(x_ref[...].astype(F32), g_ref[...]).astype(BF16)

    o_ref[...] = jnp.dot(xn_ref[...], w_ref[...], preferred_element_type=F32).astype(o_ref.dtype)


def norm_matmul(x, g, w, out_dtype, tm_pref=1024, tn_pref=1024):
    m, k = x.shape
    n = w.shape[1]
    tm, tn = _pick(m, tm_pref), _pick(n, tn_pref)
    return pl.pallas_call(
        _norm_matmul_kernel,
        grid=(m // tm, n // tn),
        in_specs=[
            pl.BlockSpec((tm, k), lambda i, j: (i, 0)),
            pl.BlockSpec((1, k), lambda i, j: (0, 0)),
            pl.BlockSpec((k, tn), lambda i, j: (0, j)),
        ],
        out_specs=pl.BlockSpec((tm, tn), lambda i, j: (i, j)),
        out_shape=jax.ShapeDtypeStruct((m, n), out_dtype),
        scratch_shapes=[pltpu.VMEM((tm, k), BF16)],
        compiler_params=_params(("parallel", "arbitrary")),
        name="norm_matmul",
    )(x, g.reshape(1, k).astype(F32), w)


def _rope128(xg, cos, sin_lo, sin_hi):
    return xg * cos + pltpu.roll(xg, 96, 1) * sin_lo + pltpu.roll(xg, 32, 1) * sin_hi


def _mla_proj_kernel(z_ref, gq_ref, gkv_ref, wq_ref, wk_ref, wv_ref, cos_ref, slo_ref, shi_ref,
                     q_ref, k_ref, v_ref, *, q_scale):
    z = z_ref[...]
    nq = _rms(z[:, :Q_LORA], gq_ref[...]).astype(BF16)
    nkv = _rms(z[:, Q_LORA:Q_LORA + KV_LORA], gkv_ref[...]).astype(BF16)
    cos, slo, shi = cos_ref[...], slo_ref[...], shi_ref[...]
    kr = _rope128(z[:, Q_LORA + KV_LORA:], cos, slo, shi).astype(BF16)
    for h in range(B_HEADS):
        qh = jnp.dot(nq, wq_ref[:, h * QK_PAD:(h + 1) * QK_PAD], preferred_element_type=F32)
        q_ref[:, h * QK_PAD:h * QK_PAD + LANES] = (qh[:, :LANES] * q_scale).astype(BF16)
        q_ref[:, h * QK_PAD + LANES:(h + 1) * QK_PAD] = (
            _rope128(qh[:, LANES:], cos, slo, shi) * q_scale).astype(BF16)
        kh = jnp.dot(nkv, wk_ref[:, h * QK_NOPE:(h + 1) * QK_NOPE], preferred_element_type=F32)
        k_ref[:, h * QK_PAD:h * QK_PAD + LANES] = kh.astype(BF16)
        k_ref[:, h * QK_PAD + LANES:(h + 1) * QK_PAD] = kr
        vh = jnp.dot(nkv, wv_ref[:, h * V_HEAD:(h + 1) * V_HEAD], preferred_element_type=F32)
        v_ref[:, h * V_HEAD:(h + 1) * V_HEAD] = vh.astype(BF16)


def mla_proj(z_lat, g_cq, g_ckv, wq, wk, wv, rope_tabs, seq, tm_pref=256):
    t = z_lat.shape[0]
    tm = _pick(min(t, seq), tm_pref)
    nseq = seq // tm
    q_scale = (QK_NOPE + QK_ROPE) ** -0.5 * math.log2(math.e)
    full = lambda a: pl.BlockSpec(a.shape, lambda i: (0, 0))
    tab = pl.BlockSpec((tm, LANES), lambda i: (i % nseq, 0))
    gq, gkv = g_cq.reshape(1, -1).astype(F32), g_ckv.reshape(1, -1).astype(F32)
    return pl.pallas_call(
        functools.partial(_mla_proj_kernel, q_scale=q_scale),
        grid=(t // tm,),
        in_specs=[pl.BlockSpec((tm, LAT_COLS), lambda i: (i, 0)), full(gq), full(gkv),
                  full(wq), full(wk), full(wv), tab, tab, tab],
        out_specs=[pl.BlockSpec((tm, B_HEADS * QK_PAD), lambda i: (i, 0)),
                   pl.BlockSpec((tm, B_HEADS * QK_PAD), lambda i: (i, 0)),
                   pl.BlockSpec((tm, B_HEADS * V_HEAD), lambda i: (i, 0))],
        out_shape=[jax.ShapeDtypeStruct((t, B_HEADS * QK_PAD), BF16),
                   jax.ShapeDtypeStruct((t, B_HEADS * QK_PAD), BF16),
                   jax.ShapeDtypeStruct((t, B_HEADS * V_HEAD), BF16)],
        compiler_params=_params(("parallel",)),
        name="mla_proj",
    )(z_lat, gq, gkv, wq, wk, wv, *rope_tabs)


def _window_kernel(q_ref, kl_ref, kc_ref, kr_ref, vl_ref, vc_ref, vr_ref, km_ref, vm_ref,
                   bw_ref, bm_ref, o_ref, *, nblk):
    n = pl.program_id(1)
    scale = A_HEAD_DIM ** -0.5
    col = lax.broadcasted_iota(jnp.int32, (1, 3 * BLOCK), 1)
    left_pen = jnp.where(n > 0, 0.0, NEG).astype(F32)
    right_pen = jnp.where(n < nblk - 1, 0.0, NEG).astype(F32)
    edge = jnp.where(col < BLOCK, left_pen, 0.0) + jnp.where(col >= 2 * BLOCK, right_pen, 0.0)
    for kh in range(A_KV_HEADS):
        cs = slice(kh * A_HEAD_DIM, (kh + 1) * A_HEAD_DIM)
        kband = jnp.concatenate([kl_ref[:, cs], kc_ref[:, cs], kr_ref[:, cs]], axis=0)
        vband = jnp.concatenate([vl_ref[:, cs], vc_ref[:, cs], vr_ref[:, cs]], axis=0)
        q4 = jnp.concatenate(
            [q_ref[:, (kh * A_GROUP + g) * A_HEAD_DIM:(kh * A_GROUP + g + 1) * A_HEAD_DIM]
             for g in range(A_GROUP)], axis=0)
        s_w = lax.dot_general(q4, kband, _NT, preferred_element_type=F32) * scale + bw_ref[kh] + edge
        s_m = lax.dot_general(q4, km_ref[:, cs], _NT, preferred_element_type=F32) * scale + bm_ref[0, kh]
        m = jnp.maximum(jnp.max(s_w, axis=-1, keepdims=True), jnp.max(s_m, axis=-1, keepdims=True))
        p_w = jnp.exp(s_w - m)
        p_m = jnp.exp(s_m - m)
        denom = jnp.sum(p_w, axis=-1, keepdims=True) + jnp.sum(p_m, axis=-1, keepdims=True)
        o = (jnp.dot(p_w.astype(BF16), vband, preferred_element_type=F32)
             + jnp.dot(p_m.astype(BF16), vm_ref[:, cs], preferred_element_type=F32)) / denom
        for g in range(A_GROUP):
            h = kh * A_GROUP + g
            o_ref[:, h * A_HEAD_DIM:(h + 1) * A_HEAD_DIM] = o[g * BLOCK:(g + 1) * BLOCK].astype(o_ref.dtype)


def window_attention(z_qkv, zm_qkv, bias_w, bias_m, bsz, seq):
    nblk = seq // BLOCK
    kcol, vcol = A_Q_COLS // A_KV_COLS, A_Q_COLS // A_KV_COLS + 1
    row = lambda b, n: b * nblk + n
    lo = lambda b, n: b * nblk + jnp.maximum(n - 1, 0)
    hi = lambda b, n: b * nblk + jnp.minimum(n + 1, nblk - 1)
    kv = lambda r, c: pl.BlockSpec((BLOCK, A_KV_COLS), lambda b, n: (r(b, n), c))
    return pl.pallas_call(
        functools.partial(_window_kernel, nblk=nblk),
        grid=(bsz, nblk),
        in_specs=[
            pl.BlockSpec((BLOCK, A_Q_COLS), lambda b, n: (row(b, n), 0)),
            kv(lo, kcol), kv(row, kcol), kv(hi, kcol),
            kv(lo, vcol), kv(row, vcol), kv(hi, vcol),
            pl.BlockSpec((META_PAD, A_KV_COLS), lambda b, n: (0, kcol)),
            pl.BlockSpec((META_PAD, A_KV_COLS), lambda b, n: (0, vcol)),
            pl.BlockSpec(bias_w.shape, lambda b, n: (0, 0, 0)),
            pl.BlockSpec((1,) + bias_m.shape[1:], lambda b, n: (jnp.minimum(n, 1), 0, 0, 0)),
        ],
        out_specs=pl.BlockSpec((BLOCK, A_Q_COLS), lambda b, n: (row(b, n), 0)),
        out_shape=jax.ShapeDtypeStruct((bsz * seq, A_Q_COLS), BF16),
        compiler_params=_params(("parallel", "arbitrary")),
        name="window_attention",
    )(z_qkv, z_qkv, z_qkv, z_qkv, z_qkv, z_qkv, z_qkv, zm_qkv, zm_qkv, bias_w, bias_m)


def _mla_kernel(q_ref, k_ref, v_ref, km_ref, vm_ref, o_ref, acc_ref, sa_ref, sb_ref, pa_ref, pb_ref,
                *, tk, npair):
    last = 2 * npair - 1

    def rows(c):
        return pl.ds(pl.multiple_of(c * tk, tk), tk)

    def scores(c):
        return lax.dot_general(q_ref[...], k_ref[rows(c), :], _NT, preferred_element_type=F32)

    def weighted(p_ref, c):
        v1 = jnp.concatenate([v_ref[rows(c), :], jnp.ones((tk, V_HEAD), BF16)], axis=1)
        return jnp.dot(p_ref[...], v1, preferred_element_type=F32)

    def softmax(s_ref, p_ref, m):
        s = s_ref[...]
        m_new = jnp.maximum(m, jnp.max(s, axis=-1, keepdims=True))
        alpha = jnp.exp2(m - m_new)
        p_ref[...] = jnp.exp2((s - m_new).astype(BF16))
        return m_new, alpha

    s0 = lax.dot_general(q_ref[...], km_ref[...], _NT, preferred_element_type=F32)
    mcol = lax.broadcasted_iota(jnp.int32, (1, META_PAD), 1)
    s0 = jnp.where(mcol < N_META, s0, NEG)
    m0 = jnp.max(s0, axis=-1, keepdims=True)
    p0 = jnp.exp2((s0 - m0).astype(BF16))
    vm1 = jnp.concatenate([vm_ref[...], jnp.ones((META_PAD, V_HEAD), BF16)], axis=1)
    acc_ref[...] = jnp.dot(p0, vm1, preferred_element_type=F32)
    sa_ref[...] = scores(0)
    pb_ref[...] = jnp.zeros_like(pb_ref)

    def body(j, carry):
        m, alpha_b = carry
        c0 = 2 * j
        sb_ref[...] = scores(c0 + 1)
        acc_ref[...] = alpha_b * acc_ref[...] + weighted(pb_ref, jnp.maximum(c0 - 1, 0))
        m, alpha_a = softmax(sa_ref, pa_ref, m)
        sa_ref[...] = scores(jnp.minimum(c0 + 2, last))
        acc_ref[...] = alpha_a * acc_ref[...] + weighted(pa_ref, c0)
        m, alpha_b = softmax(sb_ref, pb_ref, m)
        return m, alpha_b

    _, alpha_b = lax.fori_loop(0, npair, body, (m0, jnp.ones_like(m0)))
    acc = alpha_b * acc_ref[...] + weighted(pb_ref, last)
    o_ref[...] = (acc[:, :V_HEAD] / acc[:, V_HEAD:]).astype(o_ref.dtype)


def mla_attention(q, k, v, km, vm, bsz, seq, tq_pref=1024, tk_pref=1024):
    tq, tk = _pick(seq, tq_pref), _pick(seq // 2, tk_pref)
    nq = seq // tq
    return pl.pallas_call(
        functools.partial(_mla_kernel, tk=tk, npair=seq // (2 * tk)),
        grid=(bsz, B_HEADS, nq),
        in_specs=[
            pl.BlockSpec((tq, QK_PAD), lambda b, h, i: (b * nq + i, h)),
            pl.BlockSpec((seq, QK_PAD), lambda b, h, i: (b, h)),
            pl.BlockSpec((seq, V_HEAD), lambda b, h, i: (b, h)),
            pl.BlockSpec((META_PAD, QK_PAD), lambda b, h, i: (0, h)),
            pl.BlockSpec((META_PAD, V_HEAD), lambda b, h, i: (0, h)),
        ],
        out_specs=pl.BlockSpec((tq, V_HEAD), lambda b, h, i: (b * nq + i, h)),
        out_shape=jax.ShapeDtypeStruct((bsz * seq, B_HEADS * V_HEAD), BF16),
        scratch_shapes=[pltpu.VMEM((tq, 2 * V_HEAD), F32),
                        pltpu.VMEM((tq, tk), F32), pltpu.VMEM((tq, tk), F32),
                        pltpu.VMEM((tq, tk), BF16), pltpu.VMEM((tq, tk), BF16)],
        compiler_params=_params(("parallel", "parallel", "arbitrary")),
        name="mla_attention",
    )(q, k, v, km, vm)


def _merge_kernel(ya_ref, yb_ref, wa_ref, wb_ref, ga_ref, gb_ref, o_ref):
    pa = jnp.dot(ya_ref[...], wa_ref[...], preferred_element_type=F32)
    pb = jnp.dot(yb_ref[...], wb_ref[...], preferred_element_type=F32)
    o_ref[...] = (jax.nn.sigmoid(ga_ref[...].astype(F32)) * pa
                  + jax.nn.sigmoid(gb_ref[...].astype(F32)) * pb).astype(o_ref.dtype)


def gated_merge(ya, yb, w_o_a, w_o_b, gates, tm_pref=1024, tn_pref=1024):
    t = ya.shape[0]
    tm, tn = _pick(t, tm_pref), _pick(D_MODEL, tn_pref)
    nj = D_MODEL // tn
    return pl.pallas_call(
        _merge_kernel,
        grid=(t // tm, nj),
        in_specs=[
            pl.BlockSpec((tm, A_Q_COLS), lambda i, j: (i, 0)),
            pl.BlockSpec((tm, B_HEADS * V_HEAD), lambda i, j: (i, 0)),
            pl.BlockSpec((A_Q_COLS, tn), lambda i, j: (0, j)),
            pl.BlockSpec((B_HEADS * V_HEAD, tn), lambda i, j: (0, j)),
            pl.BlockSpec((tm, tn), lambda i, j: (i, j)),
            pl.BlockSpec((tm, tn), lambda i, j: (i, j + nj)),
        ],
        out_specs=pl.BlockSpec((tm, tn), lambda i, j: (i, j)),
        out_shape=jax.ShapeDtypeStruct((t, D_MODEL), BF16),
        compiler_params=_params(("parallel", "arbitrary")),
        name="gated_merge",
    )(ya, yb, w_o_a, w_o_b, gates, gates)


def _out_proj_kernel(a_ref, w_ref, x_ref, o_ref):
    o_ref[...] = x_ref[...] + jnp.dot(a_ref[...], w_ref[...], preferred_element_type=F32)


def out_proj_residual(a, w, x, tm_pref=1024, tn_pref=1024):
    t, k = a.shape
    n = w.shape[1]
    tm, tn = _pick(t, tm_pref), _pick(n, tn_pref)
    return pl.pallas_call(
        _out_proj_kernel,
        grid=(t // tm, n // tn),
        in_specs=[
            pl.BlockSpec((tm, k), lambda i, j: (i, 0)),
            pl.BlockSpec((k, tn), lambda i, j: (0, j)),
            pl.BlockSpec((tm, tn), lambda i, j: (i, j)),
        ],
        out_specs=pl.BlockSpec((tm, tn), lambda i, j: (i, j)),
        out_shape=jax.ShapeDtypeStruct((t, n), F32),
        compiler_params=_params(("parallel", "arbitrary")),
        name="out_proj_residual",
    )(a, w, x)


def _top_values(s, k):
    rows = []
    for _ in range(k):
        m = jnp.max(s, axis=0, keepdims=True)
        rows.append(m)
        s = jnp.where(s == m, -jnp.inf, s)
    return rows


def _peer_route_kernel(h_ref, g_ref, wq_ref, sk1_ref, sk2_ref,
                       hf_ref, th1_ref, r1_ref, s2_ref, e2_ref, qt_ref):
    hf = _rms(h_ref[...], g_ref[...]).astype(BF16)
    hf_ref[...] = hf
    qt_ref[...] = lax.dot_general(wq_ref[...], hf, _NT, preferred_element_type=F32).astype(BF16)
    half = PEER_KEY_DIM // 2
    tc = hf.shape[0]
    row16 = lax.broadcasted_iota(jnp.int32, (PEER_TOPK, tc), 0)
    row8 = lax.broadcasted_iota(jnp.int32, (SUBLANES, tc), 0)

    def head(h):
        base = pl.multiple_of(h * PEER_KEY_DIM, PEER_KEY_DIM)
        q1 = qt_ref[pl.ds(base, half), :]
        q2 = qt_ref[pl.ds(base + half, half), :]
        s1 = jnp.dot(sk1_ref[...], q1, preferred_element_type=F32)
        s2 = jnp.dot(sk2_ref[...], q2, preferred_element_type=F32)
        v1 = _top_values(s1, PEER_TOPK + 1)
        v2 = _top_values(s2, PEER_TOPK + 1)
        v2_all = jnp.zeros((PEER_TOPK, tc), F32)
        for r in range(PEER_TOPK):
            v2_all = jnp.where(row16 == r, v2[r], v2_all)
        v2_top = jnp.zeros((SUBLANES, tc), F32)
        v1_low = jnp.zeros((SUBLANES, tc), F32)
        for r in range(SUBLANES):
            v2_top = jnp.where(row8 == r, v2[r], v2_top)
            v1_low = jnp.where(row8 == r, v1[SUBLANES + r], v1_low)
        blocks = [v1[0] + v2_all, v1[1] + v2_top]
        for a in range(3, SUBLANES + 1):
            blocks.append(jnp.where(row8 < PEER_TOPK // a, v1[a - 1] + v2_top, -jnp.inf))
        blocks.append(v1_low + v2[0])
        cand = jnp.concatenate(blocks, axis=0)
        rem = cand
        for _ in range(PEER_TOPK):
            tau = jnp.max(rem, axis=0, keepdims=True)
            rem = jnp.where(rem == tau, -jnp.inf, rem)
        below = jnp.maximum(jnp.max(rem, axis=0, keepdims=True),
                            jnp.maximum(v1[PEER_TOPK] + v2[0], v1[0] + v2[PEER_TOPK]))
        tau = 0.5 * (tau + below)
        m1, m2 = v1[0], v2[0]
        z = jnp.sum(jnp.where(cand >= tau, jnp.exp(cand - (m1 + m2)), 0.0), axis=0, keepdims=True)
        th1_ref[h] = tau - s1
        r1_ref[h] = jnp.exp(s1 - m1) / z
        s2_ref[h] = s2
        e2_ref[h] = jnp.exp(s2 - m2)

    def head_pair(j, carry):
        head(2 * j)
        head(2 * j + 1)
        return carry

    lax.fori_loop(0, PEER_HEADS // 2, head_pair, 0)


def peer_route(h1, norm_ffn, wq_t, sk1, sk2, tc_pref=256):
    t = h1.shape[0]
    tc = _pick(t, tc_pref)
    full = lambda a: pl.BlockSpec(a.shape, lambda i: (0,) * a.ndim)
    g = norm_ffn.reshape(1, -1).astype(F32)
    keyed = pl.BlockSpec((PEER_HEADS, N_KEYS, tc), lambda i: (0, 0, i))
    keyed_shape = jax.ShapeDtypeStruct((PEER_HEADS, N_KEYS, t), F32)
    return pl.pallas_call(
        _peer_route_kernel,
        grid=(t // tc,),
        in_specs=[pl.BlockSpec((tc, D_MODEL), lambda i: (i, 0)), full(g), full(wq_t), full(sk1), full(sk2)],
        out_specs=[pl.BlockSpec((tc, D_MODEL), lambda i: (i, 0)), keyed, keyed, keyed, keyed],
        out_shape=[jax.ShapeDtypeStruct((t, D_MODEL), BF16), keyed_shape, keyed_shape, keyed_shape,
                   keyed_shape],
        scratch_shapes=[pltpu.VMEM((PEER_HEADS * PEER_KEY_DIM, tc), BF16)],
        compiler_params=_params(("parallel",)),
        name="peer_route",
    )(h1, g, wq_t, sk1, sk2)


GATE_ROWS = 32
GATE_PIECES = 8


def _peer_expert_kernel(hf_ref, dn_ref, upt_ref, th1_ref, r1_ref, th1n_ref, r1n_ref, s2_ref, e2_ref,
                        h1_ref, gf_ref, o_ref, acc_ref, ga_ref, gb_ref, wt_ref, bth_ref, br_ref,
                        *, ib, ne):
    e = pl.program_id(1)
    tc = hf_ref.shape[0]
    eb = dn_ref.shape[0]
    strip = GATE_ROWS // SUBLANES

    def gate_rows(th_ref, r_ref, g_ref, il):
        for h in range(PEER_HEADS):
            bth_ref[h, il] = jnp.broadcast_to(th_ref[h, il:il + 1, :], (SUBLANES, tc))
            br_ref[h, il] = jnp.broadcast_to(r_ref[h, il:il + 1, :], (SUBLANES, tc))
        for jr in range(N_KEYS // GATE_ROWS):
            js = slice(jr * strip, (jr + 1) * strip)
            g = jnp.zeros((strip, SUBLANES, tc), F32)
            for h in range(PEER_HEADS):
                sel = s2_ref[h, js] >= bth_ref[h, il][None]
                g = g + jnp.where(sel, e2_ref[h, js], 0.0) * br_ref[h, il][None]
            base = il * (N_KEYS // SUBLANES) + jr * strip
            g_ref[base:base + strip] = g

    @pl.when(e == 0)
    def _():
        acc_ref[...] = jnp.zeros_like(acc_ref)
        for il in range(ib):
            gate_rows(th1_ref, r1_ref, ga_ref, il)

    def block(g_cur_ref, g_next_ref):
        at = lax.dot_general(dn_ref[...], hf_ref[...], _NT, preferred_element_type=F32)
        wt_ref[...] = (0.5 * at * (1.0 + lax.erf(at * (2.0 ** -0.5)))
                       * g_cur_ref[...].reshape(eb, tc)).astype(BF16)
        rows, per = D_MODEL // GATE_PIECES, ib // GATE_PIECES
        for p in range(GATE_PIECES):
            rs = slice(p * rows, (p + 1) * rows)
            acc_ref[rs, :] += jnp.dot(upt_ref[rs, :], wt_ref[...], preferred_element_type=F32)
            for il in range(p * per, (p + 1) * per):
                gate_rows(th1n_ref, r1n_ref, g_next_ref, il)

    @pl.when(e % 2 == 0)
    def _():
        block(ga_ref, gb_ref)

    @pl.when(e % 2 == 1)
    def _():
        block(gb_ref, ga_ref)

    @pl.when(e == ne - 1)
    def _():
        o_ref[...] = _rms(h1_ref[...] + acc_ref[...].T, gf_ref[...])


def peer_experts(hf, down, up_t, th1, r1, s2, e2, h1, g_final, tc_pref=512, eb=1024):
    t = hf.shape[0]
    tc = _pick(t, tc_pref)
    ib = eb // N_KEYS
    ne = N_EXPERTS // eb
    assert ne % 2 == 0 and ib % GATE_PIECES == 0
    gf = g_final.reshape(1, -1).astype(F32)
    blk_i = pl.BlockSpec((PEER_HEADS, ib, tc), lambda c, e: (0, e, c))
    blk_n = pl.BlockSpec((PEER_HEADS, ib, tc), lambda c, e: (0, jnp.minimum(e + 1, ne - 1), c))
    s2, e2 = (a.reshape(PEER_HEADS, N_KEYS // SUBLANES, SUBLANES, t) for a in (s2, e2))
    blk_j = pl.BlockSpec((PEER_HEADS, N_KEYS // SUBLANES, SUBLANES, tc), lambda c, e: (0, 0, 0, c))
    once = pl.Buffered(1)
    return pl.pallas_call(
        functools.partial(_peer_expert_kernel, ib=ib, ne=ne),
        grid=(t // tc, ne),
        in_specs=[
            pl.BlockSpec((tc, D_MODEL), lambda c, e: (c, 0), pipeline_mode=once),
            pl.BlockSpec((eb, D_MODEL), lambda c, e: (e, 0)),
            pl.BlockSpec((D_MODEL, eb), lambda c, e: (0, e)),
            blk_i, blk_i, blk_n, blk_n, blk_j, blk_j,
            pl.BlockSpec((tc, D_MODEL), lambda c, e: (c, 0), pipeline_mode=once),
            pl.BlockSpec((1, D_MODEL), lambda c, e: (0, 0)),
        ],
        out_specs=pl.BlockSpec((tc, D_MODEL), lambda c, e: (c, 0)),
        out_shape=jax.ShapeDtypeStruct((t, D_MODEL), F32),
        scratch_shapes=[pltpu.VMEM((D_MODEL, tc), F32),
                        pltpu.VMEM((eb // SUBLANES, SUBLANES, tc), F32),
                        pltpu.VMEM((eb // SUBLANES, SUBLANES, tc), F32),
                        pltpu.VMEM((eb, tc), BF16),
                        pltpu.VMEM((PEER_HEADS, ib, SUBLANES, tc), F32),
                        pltpu.VMEM((PEER_HEADS, ib, SUBLANES, tc), F32)],
        compiler_params=_params(("parallel", "arbitrary")),
        name="peer_experts",
    )(hf, down, up_t, th1, r1, th1, r1, s2, e2, h1, gf)


def _rel_bucket(rel):
    nb = REL_BUCKETS // 2
    max_exact = nb // 2
    ret = jnp.where(rel > 0, nb, 0)
    n = jnp.abs(rel)
    nf = jnp.maximum(n, 1).astype(F32)
    large = max_exact + (jnp.log(nf / max_exact) / math.log(REL_MAX_DIST / max_exact)
                         * (nb - max_exact)).astype(jnp.int32)
    large = jnp.minimum(large, nb - 1)
    return ret + jnp.where(n < max_exact, n, large)


def _window_bias(rel_table, sink):
    i = jnp.arange(BLOCK)
    c = jnp.arange(3 * BLOCK)
    rel_w = c[None, :] - BLOCK - i[:, None]
    bw = rel_table[_rel_bucket(rel_w)].astype(F32).transpose(2, 0, 1)
    bw = jnp.where((jnp.abs(rel_w) <= WINDOW)[None], bw, NEG)
    bw = bw.reshape(A_KV_HEADS, A_GROUP * BLOCK, 3 * BLOCK)
    q_pos = N_META + jnp.arange(2)[:, None] * BLOCK + i[None, :]
    rel_m = jnp.arange(N_META)[None, None, :] - q_pos[:, :, None]
    bm = rel_table[_rel_bucket(rel_m)].astype(F32).transpose(0, 3, 1, 2)
    sink_col = jnp.broadcast_to(sink.astype(F32)[None, :, None, None], (2, A_HEADS, BLOCK, 1))
    pad = jnp.full((2, A_HEADS, BLOCK, META_PAD - N_META - 1), NEG, F32)
    bm = jnp.concatenate([bm, sink_col, pad], axis=-1)
    return bw, bm.reshape(2, A_KV_HEADS, A_GROUP * BLOCK, META_PAD)


def _rope_tables(length):
    half = QK_ROPE // 2
    freqs = ROPE_THETA ** (-jnp.arange(half, dtype=F32) / half)
    ang = jnp.arange(length).astype(F32)[:, None] * freqs[None, :]
    c, s = jnp.cos(ang), jnp.sin(ang)
    z = jnp.zeros((length, LANES - QK_ROPE), F32)
    zh = jnp.zeros((length, half), F32)
    return (jnp.concatenate([c, c, z], axis=1),
            jnp.concatenate([-s, zh, z], axis=1),
            jnp.concatenate([zh, s, z], axis=1))


def _pad_rows(a, rows):
    return jnp.pad(a, ((0, rows - a.shape[0]), (0, 0)))


def kernel(x_prompt, x_sample, meta_tokens, rel_table, norm_mix, w_in, g_cq, w_uq, g_ckv, w_ukv, attn_sink,
           w_o_a, w_o_b, w_out, norm_ffn, w_query, sub_keys1, sub_keys2, expert_down, expert_up, g_final):
    assert w_in.shape[0] == 1, "single-layer block"
    wi = w_in[0]
    c1 = QKV_COLS
    c2 = c1 + Q_LORA + KV_LORA + QK_ROPE
    w_qkv = wi[:, :c1].astype(BF16)
    w_lat = jnp.pad(wi[:, c1:c2], ((0, 0), (0, LAT_COLS - (c2 - c1)))).astype(BF16)
    w_gate = wi[:, c2:].astype(BF16)
    wq = jnp.pad(w_uq[0].reshape(Q_LORA, B_HEADS, QK_NOPE + QK_ROPE),
                 ((0, 0), (0, 0), (0, QK_PAD - QK_NOPE - QK_ROPE))).reshape(Q_LORA, B_HEADS * QK_PAD).astype(BF16)
    wkv = w_ukv[0].reshape(KV_LORA, B_HEADS, QK_NOPE + V_HEAD)
    wk = wkv[:, :, :QK_NOPE].reshape(KV_LORA, B_HEADS * QK_NOPE).astype(BF16)
    wv = wkv[:, :, QK_NOPE:].reshape(KV_LORA, B_HEADS * V_HEAD).astype(BF16)
    woa, wob, wout = w_o_a[0].astype(BF16), w_o_b[0].astype(BF16), w_out[0].astype(BF16)
    wq_t = w_query[0].T.astype(BF16)
    sk1, sk2 = sub_keys1[0].astype(BF16), sub_keys2[0].astype(BF16)
    down, up_t = expert_down[0].astype(BF16), expert_up[0].astype(BF16).T
    bias_w, bias_m = _window_bias(rel_table, attn_sink[0])

    zm_qkv = norm_matmul(meta_tokens, norm_mix[0], w_qkv, BF16)
    zm_lat = norm_matmul(meta_tokens, norm_mix[0], w_lat, F32)
    _, km, vm = mla_proj(zm_lat, g_cq[0], g_ckv[0], wq, wk, wv, _rope_tables(N_META), N_META)
    zm_qkv, km, vm = _pad_rows(zm_qkv, META_PAD), _pad_rows(km, META_PAD), _pad_rows(vm, META_PAD)

    def encode(x):
        bsz, seq, _ = x.shape
        xt = x.reshape(bsz * seq, D_MODEL)
        tabs = tuple(t[N_META:] for t in _rope_tables(N_META + seq))
        z_qkv = norm_matmul(xt, norm_mix[0], w_qkv, BF16)
        z_lat = norm_matmul(xt, norm_mix[0], w_lat, F32)
        gates = norm_matmul(xt, norm_mix[0], w_gate, BF16)
        q, k, v = mla_proj(z_lat, g_cq[0], g_ckv[0], wq, wk, wv, tabs, seq)
        y_a = window_attention(z_qkv, zm_qkv, bias_w, bias_m, bsz, seq)
        y_b = mla_attention(q, k, v, km, vm, bsz, seq)
        merged = gated_merge(y_a, y_b, woa, wob, gates)
        h1 = out_proj_residual(merged, wout, xt)
        hf, th1, r1, s2, e2 = peer_route(h1, norm_ffn[0], wq_t, sk1, sk2)
        y = peer_experts(hf, down, up_t, th1, r1, s2, e2, h1, g_final)
        return y.reshape(bsz, seq, D_MODEL)

    return encode(x_prompt), encode(x_sample)
```

```python
import functools
import math

import jax
import jax.numpy as jnp
from jax import lax
from jax.experimental import pallas as pl
from jax.experimental.pallas import tpu as pltpu

F32 = jnp.float32
BF16 = jnp.bfloat16

D_MODEL = 2048
N_META = 16
BLOCK = 128
WINDOW = 128
A_HEADS = 16
A_KV_HEADS = 4
A_HEAD_DIM = 128
A_GROUP = A_HEADS // A_KV_HEADS
B_HEADS = 16
Q_LORA = 512
KV_LORA = 256
QK_NOPE = 128
QK_ROPE = 64
V_HEAD = 128
ROPE_THETA = 10000.0
REL_BUCKETS = 32
REL_MAX_DIST = 128
PEER_HEADS = 8
N_KEYS = 128
N_EXPERTS = N_KEYS * N_KEYS
PEER_KEY_DIM = 128
PEER_TOPK = 16
EPS = 1e-6
NEG = -1e30

A_Q_COLS = A_HEADS * A_HEAD_DIM
A_KV_COLS = A_KV_HEADS * A_HEAD_DIM
QKV_COLS = A_Q_COLS + 2 * A_KV_COLS
LAT_COLS = 896
QK_PAD = 256
LANES = 128
SUBLANES = 8
META_PAD = 128
VMEM_LIMIT = 56 * 1024 * 1024

_NT = (((1,), (1,)), ((), ()))
_TN = (((0,), (0,)), ((), ()))


def _params(sem):
    return pltpu.CompilerParams(dimension_semantics=sem, vmem_limit_bytes=VMEM_LIMIT)


def _pick(n, pref):
    if n <= pref:
        return n
    t = pref
    while n % t:
        t //= 2
    return t


def _rms(xf, g):
    ms = jnp.mean(xf * xf, axis=-1, keepdims=True)
    return xf * lax.rsqrt(ms + EPS) * g


def _norm_matmul_kernel(x_ref, g_ref, w_ref, o_ref, xn_ref):
    @pl.when(pl.program_id(1) == 0)
    def _():
        xn_ref[...] = _rms(x_ref[...].astype(F32), g_ref[...]).astype(BF16)

    o_ref[...] = jnp.dot(xn_ref[...], w_ref[...], preferred_element_type=F32).astype(o_ref.dtype)


def norm_matmul(x, g, w, out_dtype, tm_pref=1024, tn_pref=1024):
    m, k = x.shape
    n = w.shape[1]
    tm, tn = _pick(m, tm_pref), _pick(n, tn_pref)
    return pl.pallas_call(
        _norm_matmul_kernel,
        grid=(m // tm, n // tn),
        in_specs=[
            pl.BlockSpec((tm, k), lambda i, j: (i, 0)),
            pl.BlockSpec((1, k), lambda i, j: (0, 0)),
            pl.BlockSpec((k, tn), lambda i, j: (0, j)),
        ],
        out_specs=pl.BlockSpec((tm, tn), lambda i, j: (i, j)),
        out_shape=jax.ShapeDtypeStruct((m, n), out_dtype),
        scratch_shapes=[pltpu.VMEM((tm, k), BF16)],
        compiler_params=_params(("parallel", "arbitrary")),
        name="norm_matmul",
    )(x, g.reshape(1, k).astype(F32), w)


def _rope128(xg, cos, sin_lo, sin_hi):
    return xg * cos + pltpu.roll(xg, 96, 1) * sin_lo + pltpu.roll(xg, 32, 1) * sin_hi


def _mla_proj_kernel(z_ref, gq_ref, gkv_ref, wq_ref, wk_ref, wv_ref, cos_ref, slo_ref, shi_ref,
                     q_ref, k_ref, v_ref, *, q_scale):
    z = z_ref[...]
    nq = _rms(z[:, :Q_LORA], gq_ref[...]).astype(BF16)
    nkv = _rms(z[:, Q_LORA:Q_LORA + KV_LORA], gkv_ref[...]).astype(BF16)
    cos, slo, shi = cos_ref[...], slo_ref[...], shi_ref[...]
    kr = _rope128(z[:, Q_LORA + KV_LORA:], cos, slo, shi).astype(BF16)
    for h in range(B_HEADS):
        qh = jnp.dot(nq, wq_ref[:, h * QK_PAD:(h + 1) * QK_PAD], preferred_element_type=F32)
        q_ref[:, h * QK_PAD:h * QK_PAD + LANES] = (qh[:, :LANES] * q_scale).astype(BF16)
        q_ref[:, h * QK_PAD + LANES:(h + 1) * QK_PAD] = (
            _rope128(qh[:, LANES:], cos, slo, shi) * q_scale).astype(BF16)
        kh = jnp.dot(nkv, wk_ref[:, h * QK_NOPE:(h + 1) * QK_NOPE], preferred_element_type=F32)
        k_ref[:, h * QK_PAD:h * QK_PAD + LANES] = kh.astype(BF16)
        k_ref[:, h * QK_PAD + LANES:(h + 1) * QK_PAD] = kr
        vh = jnp.dot(nkv, wv_ref[:, h * V_HEAD:(h + 1) * V_HEAD], preferred_element_type=F32)
        v_ref[:, h * V_HEAD:(h + 1) * V_HEAD] = vh.astype(BF16)


def mla_proj(z_lat, g_cq, g_ckv, wq, wk, wv, rope_tabs, seq, tm_pref=512):
    t = z_lat.shape[0]
    tm = _pick(min(t, seq), tm_pref)
    nseq = seq // tm
    q_scale = (QK_NOPE + QK_ROPE) ** -0.5 * math.log2(math.e)
    full = lambda a: pl.BlockSpec(a.shape, lambda i: (0, 0))
    tab = pl.BlockSpec((tm, LANES), lambda i: (i % nseq, 0))
    gq, gkv = g_cq.reshape(1, -1).astype(F32), g_ckv.reshape(1, -1).astype(F32)
    return pl.pallas_call(
        functools.partial(_mla_proj_kernel, q_scale=q_scale),
        grid=(t // tm,),
        in_specs=[pl.BlockSpec((tm, LAT_COLS), lambda i: (i, 0)), full(gq), full(gkv),
                  full(wq), full(wk), full(wv), tab, tab, tab],
        out_specs=[pl.BlockSpec((tm, B_HEADS * QK_PAD), lambda i: (i, 0)),
                   pl.BlockSpec((tm, B_HEADS * QK_PAD), lambda i: (i, 0)),
                   pl.BlockSpec((tm, B_HEADS * V_HEAD), lambda i: (i, 0))],
        out_shape=[jax.ShapeDtypeStruct((t, B_HEADS * QK_PAD), BF16),
                   jax.ShapeDtypeStruct((t, B_HEADS * QK_PAD), BF16),
                   jax.ShapeDtypeStruct((t, B_HEADS * V_HEAD), BF16)],
        compiler_params=_params(("parallel",)),
        name="mla_proj",
    )(z_lat, gq, gkv, wq, wk, wv, *rope_tabs)


def _window_kernel(q_ref, kl_ref, kc_ref, kr_ref, vl_ref, vc_ref, vr_ref, km_ref, vm_ref,
                   bw_ref, bm_ref, o_ref, *, nblk):
    n = pl.program_id(1)
    scale = A_HEAD_DIM ** -0.5
    col = lax.broadcasted_iota(jnp.int32, (1, 3 * BLOCK), 1)
    left_pen = jnp.where(n > 0, 0.0, NEG).astype(F32)
    right_pen = jnp.where(n < nblk - 1, 0.0, NEG).astype(F32)
    edge = jnp.where(col < BLOCK, left_pen, 0.0) + jnp.where(col >= 2 * BLOCK, right_pen, 0.0)
    for kh in range(A_KV_HEADS):
        cs = slice(kh * A_HEAD_DIM, (kh + 1) * A_HEAD_DIM)
        kband = jnp.concatenate([kl_ref[:, cs], kc_ref[:, cs], kr_ref[:, cs]], axis=0)
        vband = jnp.concatenate([vl_ref[:, cs], vc_ref[:, cs], vr_ref[:, cs]], axis=0)
        q4 = jnp.concatenate(
            [q_ref[:, (kh * A_GROUP + g) * A_HEAD_DIM:(kh * A_GROUP + g + 1) * A_HEAD_DIM]
             for g in range(A_GROUP)], axis=0)
        s_w = lax.dot_general(q4, kband, _NT, preferred_element_type=F32) * scale + bw_ref[kh] + edge
        s_m = lax.dot_general(q4, km_ref[:, cs], _NT, preferred_element_type=F32) * scale + bm_ref[0, kh]
        m = jnp.maximum(jnp.max(s_w, axis=-1, keepdims=True), jnp.max(s_m, axis=-1, keepdims=True))
        p_w = jnp.exp(s_w - m)
        p_m = jnp.exp(s_m - m)
        denom = jnp.sum(p_w, axis=-1, keepdims=True) + jnp.sum(p_m, axis=-1, keepdims=True)
        o = (jnp.dot(p_w.astype(BF16), vband, preferred_element_type=F32)
             + jnp.dot(p_m.astype(BF16), vm_ref[:, cs], preferred_element_type=F32)) / denom
        for g in range(A_GROUP):
            h = kh * A_GROUP + g
            o_ref[:, h * A_HEAD_DIM:(h + 1) * A_HEAD_DIM] = o[g * BLOCK:(g + 1) * BLOCK].astype(o_ref.dtype)


def window_attention(z_qkv, zm_qkv, bias_w, bias_m, bsz, seq):
    nblk = seq // BLOCK
    kcol, vcol = A_Q_COLS // A_KV_COLS, A_Q_COLS // A_KV_COLS + 1
    row = lambda b, n: b * nblk + n
    lo = lambda b, n: b * nblk + jnp.maximum(n - 1, 0)
    hi = lambda b, n: b * nblk + jnp.minimum(n + 1, nblk - 1)
    kv = lambda r, c: pl.BlockSpec((BLOCK, A_KV_COLS), lambda b, n: (r(b, n), c))
    return pl.pallas_call(
        functools.partial(_window_kernel, nblk=nblk),
        grid=(bsz, nblk),
        in_specs=[
            pl.BlockSpec((BLOCK, A_Q_COLS), lambda b, n: (row(b, n), 0)),
            kv(lo, kcol), kv(row, kcol), kv(hi, kcol),
            kv(lo, vcol), kv(row, vcol), kv(hi, vcol),
            pl.BlockSpec((META_PAD, A_KV_COLS), lambda b, n: (0, kcol)),
            pl.BlockSpec((META_PAD, A_KV_COLS), lambda b, n: (0, vcol)),
            pl.BlockSpec(bias_w.shape, lambda b, n: (0, 0, 0)),
            pl.BlockSpec((1,) + bias_m.shape[1:], lambda b, n: (jnp.minimum(n, 1), 0, 0, 0)),
        ],
        out_specs=pl.BlockSpec((BLOCK, A_Q_COLS), lambda b, n: (row(b, n), 0)),
        out_shape=jax.ShapeDtypeStruct((bsz * seq, A_Q_COLS), BF16),
        compiler_params=_params(("parallel", "arbitrary")),
        name="window_attention",
    )(z_qkv, z_qkv, z_qkv, z_qkv, z_qkv, z_qkv, z_qkv, zm_qkv, zm_qkv, bias_w, bias_m)


def _mla_kernel(q_ref, k_ref, v_ref, km_ref, vm_ref, o_ref, acc_ref, sa_ref, sb_ref, pa_ref, pb_ref,
                *, tk, npair):
    last = 2 * npair - 1

    def rows(c):
        return pl.ds(pl.multiple_of(c * tk, tk), tk)

    def scores(c):
        return lax.dot_general(q_ref[...], k_ref[rows(c), :], _NT, preferred_element_type=F32)

    def weighted(p_ref, c):
        v1 = jnp.concatenate([v_ref[rows(c), :], jnp.ones((tk, V_HEAD), BF16)], axis=1)
        return jnp.dot(p_ref[...], v1, preferred_element_type=F32)

    def softmax(s_ref, p_ref, m):
        s = s_ref[...]
        m_new = jnp.maximum(m, jnp.max(s, axis=-1, keepdims=True))
        alpha = jnp.exp2(m - m_new)
        p_ref[...] = jnp.exp2((s - m_new).astype(BF16))
        return m_new, alpha

    s0 = lax.dot_general(q_ref[...], km_ref[...], _NT, preferred_element_type=F32)
    mcol = lax.broadcasted_iota(jnp.int32, (1, META_PAD), 1)
    s0 = jnp.where(mcol < N_META, s0, NEG)
    m0 = jnp.max(s0, axis=-1, keepdims=True)
    p0 = jnp.exp2((s0 - m0).astype(BF16))
    vm1 = jnp.concatenate([vm_ref[...], jnp.ones((META_PAD, V_HEAD), BF16)], axis=1)
    acc_ref[...] = jnp.dot(p0, vm1, preferred_element_type=F32)
    sa_ref[...] = scores(0)
    pb_ref[...] = jnp.zeros_like(pb_ref)

    def body(j, carry):
        m, alpha_b = carry
        c0 = 2 * j
        sb_ref[...] = scores(c0 + 1)
        acc_ref[...] = alpha_b * acc_ref[...] + weighted(pb_ref, jnp.maximum(c0 - 1, 0))
        m, alpha_a = softmax(sa_ref, pa_ref, m)
        sa_ref[...] = scores(jnp.minimum(c0 + 2, last))
        acc_ref[...] = alpha_a * acc_ref[...] + weighted(pa_ref, c0)
        m, alpha_b = softmax(sb_ref, pb_ref, m)
        return m, alpha_b

    _, alpha_b = lax.fori_loop(0, npair, body, (m0, jnp.ones_like(m0)))
    acc = alpha_b * acc_ref[...] + weighted(pb_ref, last)
    o_ref[...] = (acc[:, :V_HEAD] / acc[:, V_HEAD:]).astype(o_ref.dtype)


def mla_attention(q, k, v, km, vm, bsz, seq, tq_pref=1024, tk_pref=1024):
    tq, tk = _pick(seq, tq_pref), _pick(seq // 2, tk_pref)
    nq = seq // tq
    return pl.pallas_call(
        functools.partial(_mla_kernel, tk=tk, npair=seq // (2 * tk)),
        grid=(bsz, B_HEADS, nq),
        in_specs=[
            pl.BlockSpec((tq, QK_PAD), lambda b, h, i: (b * nq + i, h)),
            pl.BlockSpec((seq, QK_PAD), lambda b, h, i: (b, h)),
            pl.BlockSpec((seq, V_HEAD), lambda b, h, i: (b, h)),
            pl.BlockSpec((META_PAD, QK_PAD), lambda b, h, i: (0, h)),
            pl.BlockSpec((META_PAD, V_HEAD), lambda b, h, i: (0, h)),
        ],
        out_specs=pl.BlockSpec((tq, V_HEAD), lambda b, h, i: (b * nq + i, h)),
        out_shape=jax.ShapeDtypeStruct((bsz * seq, B_HEADS * V_HEAD), BF16),
        scratch_shapes=[pltpu.VMEM((tq, 2 * V_HEAD), F32),
                        pltpu.VMEM((tq, tk), F32), pltpu.VMEM((tq, tk), F32),
                        pltpu.VMEM((tq, tk), BF16), pltpu.VMEM((tq, tk), BF16)],
        compiler_params=_params(("parallel", "parallel", "arbitrary")),
        name="mla_attention",
    )(q, k, v, km, vm)


def _merge_kernel(ya_ref, yb_ref, wa_ref, wb_ref, ga_ref, gb_ref, o_ref):
    pa = jnp.dot(ya_ref[...], wa_ref[...], preferred_element_type=F32)
    pb = jnp.dot(yb_ref[...], wb_ref[...], preferred_element_type=F32)
    o_ref[...] = (jax.nn.sigmoid(ga_ref[...].astype(F32)) * pa
                  + jax.nn.sigmoid(gb_ref[...].astype(F32)) * pb).astype(o_ref.dtype)


def gated_merge(ya, yb, w_o_a, w_o_b, gates, tm_pref=1024, tn_pref=1024):
    t = ya.shape[0]
    tm, tn = _pick(t, tm_pref), _pick(D_MODEL, tn_pref)
    nj = D_MODEL // tn
    return pl.pallas_call(
        _merge_kernel,
        grid=(t // tm, nj),
        in_specs=[
            pl.BlockSpec((tm, A_Q_COLS), lambda i, j: (i, 0)),
            pl.BlockSpec((tm, B_HEADS * V_HEAD), lambda i, j: (i, 0)),
            pl.BlockSpec((A_Q_COLS, tn), lambda i, j: (0, j)),
            pl.BlockSpec((B_HEADS * V_HEAD, tn), lambda i, j: (0, j)),
            pl.BlockSpec((tm, tn), lambda i, j: (i, j)),
            pl.BlockSpec((tm, tn), lambda i, j: (i, j + nj)),
        ],
        out_specs=pl.BlockSpec((tm, tn), lambda i, j: (i, j)),
        out_shape=jax.ShapeDtypeStruct((t, D_MODEL), BF16),
        compiler_params=_params(("parallel", "arbitrary")),
        name="gated_merge",
    )(ya, yb, w_o_a, w_o_b, gates, gates)


def _out_proj_kernel(a_ref, w_ref, x_ref, o_ref):
    o_ref[...] = x_ref[...] + jnp.dot(a_ref[...], w_ref[...], preferred_element_type=F32)


def out_proj_residual(a, w, x, tm_pref=1024, tn_pref=1024):
    t, k = a.shape
    n = w.shape[1]
    tm, tn = _pick(t, tm_pref), _pick(n, tn_pref)
    return pl.pallas_call(
        _out_proj_kernel,
        grid=(t // tm, n // tn),
        in_specs=[
            pl.BlockSpec((tm, k), lambda i, j: (i, 0)),
            pl.BlockSpec((k, tn), lambda i, j: (0, j)),
            pl.BlockSpec((tm, tn), lambda i, j: (i, j)),
        ],
        out_specs=pl.BlockSpec((tm, tn), lambda i, j: (i, j)),
        out_shape=jax.ShapeDtypeStruct((t, n), F32),
        compiler_params=_params(("parallel", "arbitrary")),
        name="out_proj_residual",
    )(a, w, x)


def _top_values(s, k):
    rows = []
    for _ in range(k):
        m = jnp.max(s, axis=0, keepdims=True)
        rows.append(m)
        s = jnp.where(s == m, -jnp.inf, s)
    return rows


def _peer_route_kernel(h_ref, g_ref, wq_ref, sk1_ref, sk2_ref,
                       hf_ref, th1_ref, r1_ref, s2_ref, e2_ref, qt_ref):
    hf = _rms(h_ref[...], g_ref[...]).astype(BF16)
    hf_ref[...] = hf
    qt_ref[...] = lax.dot_general(wq_ref[...], hf, _NT, preferred_element_type=F32).astype(BF16)
    half = PEER_KEY_DIM // 2
    tc = hf.shape[0]
    row16 = lax.broadcasted_iota(jnp.int32, (PEER_TOPK, tc), 0)
    row8 = lax.broadcasted_iota(jnp.int32, (SUBLANES, tc), 0)

    def head(h):
        base = pl.multiple_of(h * PEER_KEY_DIM, PEER_KEY_DIM)
        q1 = qt_ref[pl.ds(base, half), :]
        q2 = qt_ref[pl.ds(base + half, half), :]
        s1 = jnp.dot(sk1_ref[...], q1, preferred_element_type=F32)
        s2 = jnp.dot(sk2_ref[...], q2, preferred_element_type=F32)
        v1 = _top_values(s1, PEER_TOPK + 1)
        v2 = _top_values(s2, PEER_TOPK + 1)
        v2_all = jnp.zeros((PEER_TOPK, tc), F32)
        for r in range(PEER_TOPK):
            v2_all = jnp.where(row16 == r, v2[r], v2_all)
        v2_top = jnp.zeros((SUBLANES, tc), F32)
        v1_low = jnp.zeros((SUBLANES, tc), F32)
        for r in range(SUBLANES):
            v2_top = jnp.where(row8 == r, v2[r], v2_top)
            v1_low = jnp.where(row8 == r, v1[SUBLANES + r], v1_low)
        blocks = [v1[0] + v2_all, v1[1] + v2_top]
        for a in range(3, SUBLANES + 1):
            blocks.append(jnp.where(row8 < PEER_TOPK // a, v1[a - 1] + v2_top, -jnp.inf))
        blocks.append(v1_low + v2[0])
        cand = jnp.concatenate(blocks, axis=0)
        rem = cand
        for _ in range(PEER_TOPK):
            tau = jnp.max(rem, axis=0, keepdims=True)
            rem = jnp.where(rem == tau, -jnp.inf, rem)
        below = jnp.maximum(jnp.max(rem, axis=0, keepdims=True),
                            jnp.maximum(v1[PEER_TOPK] + v2[0], v1[0] + v2[PEER_TOPK]))
        tau = 0.5 * (tau + below)
        m1, m2 = v1[0], v2[0]
        z = jnp.sum(jnp.where(cand >= tau, jnp.exp(cand - (m1 + m2)), 0.0), axis=0, keepdims=True)
        th1_ref[h] = tau - s1
        r1_ref[h] = jnp.exp(s1 - m1) / z
        s2_ref[h] = s2
        e2_ref[h] = jnp.exp(s2 - m2)

    def head_pair(j, carry):
        head(2 * j)
        head(2 * j + 1)
        return carry

    lax.fori_loop(0, PEER_HEADS // 2, head_pair, 0)


def peer_route(h1, norm_ffn, wq_t, sk1, sk2, tc_pref=256):
    t = h1.shape[0]
    tc = _pick(t, tc_pref)
    full = lambda a: pl.BlockSpec(a.shape, lambda i: (0,) * a.ndim)
    g = norm_ffn.reshape(1, -1).astype(F32)
    keyed = pl.BlockSpec((PEER_HEADS, N_KEYS, tc), lambda i: (0, 0, i))
    keyed_shape = jax.ShapeDtypeStruct((PEER_HEADS, N_KEYS, t), F32)
    return pl.pallas_call(
        _peer_route_kernel,
        grid=(t // tc,),
        in_specs=[pl.BlockSpec((tc, D_MODEL), lambda i: (i, 0)), full(g), full(wq_t), full(sk1), full(sk2)],
        out_specs=[pl.BlockSpec((tc, D_MODEL), lambda i: (i, 0)), keyed, keyed, keyed, keyed],
        out_shape=[jax.ShapeDtypeStruct((t, D_MODEL), BF16), keyed_shape, keyed_shape, keyed_shape,
                   keyed_shape],
        scratch_shapes=[pltpu.VMEM((PEER_HEADS * PEER_KEY_DIM, tc), BF16)],
        compiler_params=_params(("parallel",)),
        name="peer_route",
    )(h1, g, wq_t, sk1, sk2)


GATE_ROWS = 32
GATE_PIECES = 8


def _peer_expert_kernel(hf_ref, dn_ref, upt_ref, th1_ref, r1_ref, th1n_ref, r1n_ref, s2_ref, e2_ref,
                        h1_ref, gf_ref, o_ref, acc_ref, ga_ref, gb_ref, wt_ref, bth_ref, br_ref,
                        *, ib, ne):
    e = pl.program_id(1)
    tc = hf_ref.shape[0]
    eb = dn_ref.shape[0]
    strip = GATE_ROWS // SUBLANES

    def gate_rows(th_ref, r_ref, g_ref, il):
        for h in range(PEER_HEADS):
            bth_ref[h, il] = jnp.broadcast_to(th_ref[h, il:il + 1, :], (SUBLANES, tc))
            br_ref[h, il] = jnp.broadcast_to(r_ref[h, il:il + 1, :], (SUBLANES, tc))
        for jr in range(N_KEYS // GATE_ROWS):
            js = slice(jr * strip, (jr + 1) * strip)
            g = jnp.zeros((strip, SUBLANES, tc), F32)
            for h in range(PEER_HEADS):
                sel = s2_ref[h, js] >= bth_ref[h, il][None]
                g = g + jnp.where(sel, e2_ref[h, js], 0.0) * br_ref[h, il][None]
            base = il * (N_KEYS // SUBLANES) + jr * strip
            g_ref[base:base + strip] = g

    @pl.when(e == 0)
    def _():
        acc_ref[...] = jnp.zeros_like(acc_ref)
        for il in range(ib):
            gate_rows(th1_ref, r1_ref, ga_ref, il)

    def block(g_cur_ref, g_next_ref):
        at = lax.dot_general(dn_ref[...], hf_ref[...], _NT, preferred_element_type=F32)
        wt_ref[...] = (0.5 * at * (1.0 + lax.erf(at * (2.0 ** -0.5)))
                       * g_cur_ref[...].reshape(eb, tc)).astype(BF16)
        rows, per = D_MODEL // GATE_PIECES, ib // GATE_PIECES
        for p in range(GATE_PIECES):
            rs = slice(p * rows, (p + 1) * rows)
            acc_ref[rs, :] += jnp.dot(upt_ref[rs, :], wt_ref[...], preferred_element_type=F32)
            for il in range(p * per, (p + 1) * per):
                gate_rows(th1n_ref, r1n_ref, g_next_ref, il)

    @pl.when(e % 2 == 0)
    def _():
        block(ga_ref, gb_ref)

    @pl.when(e % 2 == 1)
    def _():
        block(gb_ref, ga_ref)

    @pl.when(e == ne - 1)
    def _():
        o_ref[...] = _rms(h1_ref[...] + acc_ref[...].T, gf_ref[...])


def peer_experts(hf, down, up_t, th1, r1, s2, e2, h1, g_final, tc_pref=512, eb=1024):
    t = hf.shape[0]
    tc = _pick(t, tc_pref)
    ib = eb // N_KEYS
    ne = N_EXPERTS // eb
    assert ne % 2 == 0 and ib % GATE_PIECES == 0
    gf = g_final.reshape(1, -1).astype(F32)
    blk_i = pl.BlockSpec((PEER_HEADS, ib, tc), lambda c, e: (0, e, c))
    blk_n = pl.BlockSpec((PEER_HEADS, ib, tc), lambda c, e: (0, jnp.minimum(e + 1, ne - 1), c))
    s2, e2 = (a.reshape(PEER_HEADS, N_KEYS // SUBLANES, SUBLANES, t) for a in (s2, e2))
    blk_j = pl.BlockSpec((PEER_HEADS, N_KEYS // SUBLANES, SUBLANES, tc), lambda c, e: (0, 0, 0, c))
    once = pl.Buffered(1)
    return pl.pallas_call(
        functools.partial(_peer_expert_kernel, ib=ib, ne=ne),
        grid=(t // tc, ne),
        in_specs=[
            pl.BlockSpec((tc, D_MODEL), lambda c, e: (c, 0), pipeline_mode=once),
            pl.BlockSpec((eb, D_MODEL), lambda c, e: (e, 0)),
            pl.BlockSpec((D_MODEL, eb), lambda c, e: (0, e)),
            blk_i, blk_i, blk_n, blk_n, blk_j, blk_j,
            pl.BlockSpec((tc, D_MODEL), lambda c, e: (c, 0), pipeline_mode=once),
            pl.BlockSpec((1, D_MODEL), lambda c, e: (0, 0)),
        ],
        out_specs=pl.BlockSpec((tc, D_MODEL), lambda c, e: (c, 0)),
        out_shape=jax.ShapeDtypeStruct((t, D_MODEL), F32),
        scratch_shapes=[pltpu.VMEM((D_MODEL, tc), F32),
                        pltpu.VMEM((eb // SUBLANES, SUBLANES, tc), F32),
                        pltpu.VMEM((eb // SUBLANES, SUBLANES, tc), F32),
                        pltpu.VMEM((eb, tc), BF16),
                        pltpu.VMEM((PEER_HEADS, ib, SUBLANES, tc), F32),
                        pltpu.VMEM((PEER_HEADS, ib, SUBLANES, tc), F32)],
        compiler_params=_params(("parallel", "arbitrary")),
        name="peer_experts",
    )(hf, down, up_t, th1, r1, th1, r1, s2, e2, h1, gf)


def _rel_bucket(rel):
    nb = REL_BUCKETS // 2
    max_exact = nb // 2
    ret = jnp.where(rel > 0, nb, 0)
    n = jnp.abs(rel)
    nf = jnp.maximum(n, 1).astype(F32)
    large = max_exact + (jnp.log(nf / max_exact) / math.log(REL_MAX_DIST / max_exact)
                         * (nb - max_exact)).astype(jnp.int32)
    large = jnp.minimum(large, nb - 1)
    return ret + jnp.where(n < max_exact, n, large)


def _window_bias(rel_table, sink):
    i = jnp.arange(BLOCK)
    rel = jnp.arange(-(2 * BLOCK - 1), 2 * BLOCK)
    by_rel = jnp.where((jnp.abs(rel) <= WINDOW)[:, None], rel_table[_rel_bucket(rel)].astype(F32), NEG).T
    bw = jnp.stack([by_rel[:, BLOCK - 1 - q:BLOCK - 1 - q + 3 * BLOCK] for q in range(BLOCK)], axis=1)
    bw = bw.reshape(A_KV_HEADS, A_GROUP * BLOCK, 3 * BLOCK)
    q_pos = N_META + jnp.arange(2)[:, None] * BLOCK + i[None, :]
    rel_m = jnp.arange(N_META)[None, None, :] - q_pos[:, :, None]
    bm = rel_table[_rel_bucket(rel_m)].astype(F32).transpose(0, 3, 1, 2)
    sink_col = jnp.broadcast_to(sink.astype(F32)[None, :, None, None], (2, A_HEADS, BLOCK, 1))
    pad = jnp.full((2, A_HEADS, BLOCK, META_PAD - N_META - 1), NEG, F32)
    bm = jnp.concatenate([bm, sink_col, pad], axis=-1)
    return bw, bm.reshape(2, A_KV_HEADS, A_GROUP * BLOCK, META_PAD)


def _rope_tables(length):
    half = QK_ROPE // 2
    freqs = ROPE_THETA ** (-jnp.arange(half, dtype=F32) / half)
    ang = jnp.arange(length).astype(F32)[:, None] * freqs[None, :]
    c, s = jnp.cos(ang), jnp.sin(ang)
    z = jnp.zeros((length, LANES - QK_ROPE), F32)
    zh = jnp.zeros((length, half), F32)
    return (jnp.concatenate([c, c, z], axis=1),
            jnp.concatenate([-s, zh, z], axis=1),
            jnp.concatenate([zh, s, z], axis=1))


def _pad_rows(a, rows):
    return jnp.pad(a, ((0, rows - a.shape[0]), (0, 0)))


def kernel(x_prompt, x_sample, meta_tokens, rel_table, norm_mix, w_in, g_cq, w_uq, g_ckv, w_ukv, attn_sink,
           w_o_a, w_o_b, w_out, norm_ffn, w_query, sub_keys1, sub_keys2, expert_down, expert_up, g_final):
    assert w_in.shape[0] == 1, "single-layer block"
    wi = w_in[0]
    c1 = QKV_COLS
    c2 = c1 + Q_LORA + KV_LORA + QK_ROPE
    w_qkv = wi[:, :c1].astype(BF16)
    w_lat = jnp.pad(wi[:, c1:c2], ((0, 0), (0, LAT_COLS - (c2 - c1)))).astype(BF16)
    w_gate = wi[:, c2:].astype(BF16)
    wq = jnp.pad(w_uq[0].reshape(Q_LORA, B_HEADS, QK_NOPE + QK_ROPE),
                 ((0, 0), (0, 0), (0, QK_PAD - QK_NOPE - QK_ROPE))).reshape(Q_LORA, B_HEADS * QK_PAD).astype(BF16)
    wkv = w_ukv[0].reshape(KV_LORA, B_HEADS, QK_NOPE + V_HEAD)
    wk = wkv[:, :, :QK_NOPE].reshape(KV_LORA, B_HEADS * QK_NOPE).astype(BF16)
    wv = wkv[:, :, QK_NOPE:].reshape(KV_LORA, B_HEADS * V_HEAD).astype(BF16)
    woa, wob, wout = w_o_a[0].astype(BF16), w_o_b[0].astype(BF16), w_out[0].astype(BF16)
    wq_t = w_query[0].T.astype(BF16)
    sk1, sk2 = sub_keys1[0].astype(BF16), sub_keys2[0].astype(BF16)
    down, up_t = expert_down[0].astype(BF16), expert_up[0].astype(BF16).T
    bias_w, bias_m = _window_bias(rel_table, attn_sink[0])

    zm_qkv = norm_matmul(meta_tokens, norm_mix[0], w_qkv, BF16)
    zm_lat = norm_matmul(meta_tokens, norm_mix[0], w_lat, F32)
    _, km, vm = mla_proj(zm_lat, g_cq[0], g_ckv[0], wq, wk, wv, _rope_tables(N_META), N_META)
    zm_qkv, km, vm = _pad_rows(zm_qkv, META_PAD), _pad_rows(km, META_PAD), _pad_rows(vm, META_PAD)

    def encode(x):
        bsz, seq, _ = x.shape
        xt = x.reshape(bsz * seq, D_MODEL)
        tabs = tuple(t[N_META:] for t in _rope_tables(N_META + seq))
        z_qkv = norm_matmul(xt, norm_mix[0], w_qkv, BF16)
        z_lat = norm_matmul(xt, norm_mix[0], w_lat, F32)
        gates = norm_matmul(xt, norm_mix[0], w_gate, BF16)
        q, k, v = mla_proj(z_lat, g_cq[0], g_ckv[0], wq, wk, wv, tabs, seq)
        y_a = window_attention(z_qkv, zm_qkv, bias_w, bias_m, bsz, seq)
        y_b = mla_attention(q, k, v, km, vm, bsz, seq)
        merged = gated_merge(y_a, y_b, woa, wob, gates)
        h1 = out_proj_residual(merged, wout, xt)
        hf, th1, r1, s2, e2 = peer_route(h1, norm_ffn[0], wq_t, sk1, sk2)
        y = peer_experts(hf, down, up_t, th1, r1, s2, e2, h1, g_final)
        return y.reshape(bsz, seq, D_MODEL)

    return encode(x_prompt), encode(x_sample)
```

```python
import functools
import math

import jax
import jax.numpy as jnp
from jax import lax
from jax.experimental import pallas as pl
from jax.experimental.pallas import tpu as pltpu

F32 = jnp.float32
BF16 = jnp.bfloat16

D_MODEL = 2048
N_META = 16
BLOCK = 128
WINDOW = 128
A_HEADS = 16
A_KV_HEADS = 4
A_HEAD_DIM = 128
A_GROUP = A_HEADS // A_KV_HEADS
B_HEADS = 16
Q_LORA = 512
KV_LORA = 256
QK_NOPE = 128
QK_ROPE = 64
V_HEAD = 128
ROPE_THETA = 10000.0
REL_BUCKETS = 32
REL_MAX_DIST = 128
PEER_HEADS = 8
N_KEYS = 128
N_EXPERTS = N_KEYS * N_KEYS
PEER_KEY_DIM = 128
PEER_TOPK = 16
EPS = 1e-6
NEG = -1e30

A_Q_COLS = A_HEADS * A_HEAD_DIM
A_KV_COLS = A_KV_HEADS * A_HEAD_DIM
QKV_COLS = A_Q_COLS + 2 * A_KV_COLS
LAT_COLS = 896
QK_PAD = 256
LANES = 128
SUBLANES = 8
META_PAD = 128
VMEM_LIMIT = 56 * 1024 * 1024

_NT = (((1,), (1,)), ((), ()))
_TN = (((0,), (0,)), ((), ()))


def _params(sem):
    return pltpu.CompilerParams(dimension_semantics=sem, vmem_limit_bytes=VMEM_LIMIT)


def _pick(n, pref):
    if n <= pref:
        return n
    t = pref
    while n % t:
        t //= 2
    return t


def _rms(xf, g):
    ms = jnp.mean(xf * xf, axis=-1, keepdims=True)
    return xf * lax.rsqrt(ms + EPS) * g


def _norm_matmul_kernel(x_ref, g_ref, w_ref, o_ref, xn_ref):
    @pl.when(pl.program_id(1) == 0)
    def _():
        xn_ref[...] = _rms(x_ref[...].astype(F32), g_ref[...]).astype(BF16)

    o_ref[...] = jnp.dot(xn_ref[...], w_ref[...], preferred_element_type=F32).astype(o_ref.dtype)


def norm_matmul(x, g, w, out_dtype, tm_pref=1024, tn_pref=1024):
    m, k = x.shape
    n = w.shape[1]
    tm, tn = _pick(m, tm_pref), _pick(n, tn_pref)
    return pl.pallas_call(
        _norm_matmul_kernel,
        grid=(m // tm, n // tn),
        in_specs=[
            pl.BlockSpec((tm, k), lambda i, j: (i, 0)),
            pl.BlockSpec((1, k), lambda i, j: (0, 0)),
            pl.BlockSpec((k, tn), lambda i, j: (0, j)),
        ],
        out_specs=pl.BlockSpec((tm, tn), lambda i, j: (i, j)),
        out_shape=jax.ShapeDtypeStruct((m, n), out_dtype),
        scratch_shapes=[pltpu.VMEM((tm, k), BF16)],
        compiler_params=_params(("parallel", "arbitrary")),
        name="norm_matmul",
    )(x, g.reshape(1, k).astype(F32), w)


def _rope128(xg, cos, sin_lo, sin_hi):
    return xg * cos + pltpu.roll(xg, 96, 1) * sin_lo + pltpu.roll(xg, 32, 1) * sin_hi


def _mla_proj_kernel(z_ref, gq_ref, gkv_ref, wq_ref, wk_ref, wv_ref, cos_ref, slo_ref, shi_ref,
                     q_ref, k_ref, v_ref, *, q_scale):
    z = z_ref[...]
    nq = _rms(z[:, :Q_LORA], gq_ref[...]).astype(BF16)
    nkv = _rms(z[:, Q_LORA:Q_LORA + KV_LORA], gkv_ref[...]).astype(BF16)
    cos, slo, shi = cos_ref[...], slo_ref[...], shi_ref[...]
    kr = _rope128(z[:, Q_LORA + KV_LORA:], cos, slo, shi).astype(BF16)
    for h in range(B_HEADS):
        qh = jnp.dot(nq, wq_ref[:, h * QK_PAD:(h + 1) * QK_PAD], preferred_element_type=F32)
        q_ref[:, h * QK_PAD:h * QK_PAD + LANES] = (qh[:, :LANES] * q_scale).astype(BF16)
        q_ref[:, h * QK_PAD + LANES:(h + 1) * QK_PAD] = (
            _rope128(qh[:, LANES:], cos, slo, shi) * q_scale).astype(BF16)
        kh = jnp.dot(nkv, wk_ref[:, h * QK_NOPE:(h + 1) * QK_NOPE], preferred_element_type=F32)
        k_ref[:, h * QK_PAD:h * QK_PAD + LANES] = kh.astype(BF16)
        k_ref[:, h * QK_PAD + LANES:(h + 1) * QK_PAD] = kr
        vh = jnp.dot(nkv, wv_ref[:, h * V_HEAD:(h + 1) * V_HEAD], preferred_element_type=F32)
        v_ref[:, h * V_HEAD:(h + 1) * V_HEAD] = vh.astype(BF16)


def mla_proj(z_lat, g_cq, g_ckv, wq, wk, wv, rope_tabs, seq, tm_pref=256):
    t = z_lat.shape[0]
    tm = _pick(min(t, seq), tm_pref)
    nseq = seq // tm
    q_scale = (QK_NOPE + QK_ROPE) ** -0.5 * math.log2(math.e)
    full = lambda a: pl.BlockSpec(a.shape, lambda i: (0, 0))
    tab = pl.BlockSpec((tm, LANES), lambda i: (i % nseq, 0))
    gq, gkv = g_cq.reshape(1, -1).astype(F32), g_ckv.reshape(1, -1).astype(F32)
    return pl.pallas_call(
        functools.partial(_mla_proj_kernel, q_scale=q_scale),
        grid=(t // tm,),
        in_specs=[pl.BlockSpec((tm, LAT_COLS), lambda i: (i, 0)), full(gq), full(gkv),
                  full(wq), full(wk), full(wv), tab, tab, tab],
        out_specs=[pl.BlockSpec((tm, B_HEADS * QK_PAD), lambda i: (i, 0)),
                   pl.BlockSpec((tm, B_HEADS * QK_PAD), lambda i: (i, 0)),
                   pl.BlockSpec((tm, B_HEADS * V_HEAD), lambda i: (i, 0))],
        out_shape=[jax.ShapeDtypeStruct((t, B_HEADS * QK_PAD), BF16),
                   jax.ShapeDtypeStruct((t, B_HEADS * QK_PAD), BF16),
                   jax.ShapeDtypeStruct((t, B_HEADS * V_HEAD), BF16)],
        compiler_params=_params(("parallel",)),
        name="mla_proj",
    )(z_lat, gq, gkv, wq, wk, wv, *rope_tabs)


def _window_kernel(q_ref, kl_ref, kc_ref, kr_ref, vl_ref, vc_ref, vr_ref, km_ref, vm_ref,
                   bw_ref, bm_ref, o_ref, *, nblk):
    n = pl.program_id(1)
    scale = A_HEAD_DIM ** -0.5
    col = lax.broadcasted_iota(jnp.int32, (1, 3 * BLOCK), 1)
    left_pen = jnp.where(n > 0, 0.0, NEG).astype(F32)
    right_pen = jnp.where(n < nblk - 1, 0.0, NEG).astype(F32)
    edge = jnp.where(col < BLOCK, left_pen, 0.0) + jnp.where(col >= 2 * BLOCK, right_pen, 0.0)
    for kh in range(A_KV_HEADS):
        cs = slice(kh * A_HEAD_DIM, (kh + 1) * A_HEAD_DIM)
        keys = jnp.concatenate([kl_ref[:, cs], kc_ref[:, cs], kr_ref[:, cs], km_ref[:, cs]], axis=0)
        vals = jnp.concatenate([vl_ref[:, cs], vc_ref[:, cs], vr_ref[:, cs], vm_ref[:, cs]], axis=0)
        q4 = jnp.concatenate(
            [q_ref[:, (kh * A_GROUP + g) * A_HEAD_DIM:(kh * A_GROUP + g + 1) * A_HEAD_DIM]
             for g in range(A_GROUP)], axis=0)
        bias = jnp.concatenate([bw_ref[kh] + edge, bm_ref[0, kh]], axis=1)
        s = lax.dot_general(q4, keys, _NT, preferred_element_type=F32) * scale + bias
        p = jnp.exp(s - jnp.max(s, axis=-1, keepdims=True))
        o = jnp.dot(p.astype(BF16), vals, preferred_element_type=F32) / jnp.sum(p, axis=-1, keepdims=True)
        for g in range(A_GROUP):
            h = kh * A_GROUP + g
            o_ref[:, h * A_HEAD_DIM:(h + 1) * A_HEAD_DIM] = o[g * BLOCK:(g + 1) * BLOCK].astype(o_ref.dtype)


def window_attention(z_qkv, zm_qkv, bias_w, bias_m, bsz, seq):
    nblk = seq // BLOCK
    kcol, vcol = A_Q_COLS // A_KV_COLS, A_Q_COLS // A_KV_COLS + 1
    row = lambda b, n: b * nblk + n
    lo = lambda b, n: b * nblk + jnp.maximum(n - 1, 0)
    hi = lambda b, n: b * nblk + jnp.minimum(n + 1, nblk - 1)
    kv = lambda r, c: pl.BlockSpec((BLOCK, A_KV_COLS), lambda b, n: (r(b, n), c))
    return pl.pallas_call(
        functools.partial(_window_kernel, nblk=nblk),
        grid=(bsz, nblk),
        in_specs=[
            pl.BlockSpec((BLOCK, A_Q_COLS), lambda b, n: (row(b, n), 0)),
            kv(lo, kcol), kv(row, kcol), kv(hi, kcol),
            kv(lo, vcol), kv(row, vcol), kv(hi, vcol),
            pl.BlockSpec((META_PAD, A_KV_COLS), lambda b, n: (0, kcol)),
            pl.BlockSpec((META_PAD, A_KV_COLS), lambda b, n: (0, vcol)),
            pl.BlockSpec(bias_w.shape, lambda b, n: (0, 0, 0)),
            pl.BlockSpec((1,) + bias_m.shape[1:], lambda b, n: (jnp.minimum(n, 1), 0, 0, 0)),
        ],
        out_specs=pl.BlockSpec((BLOCK, A_Q_COLS), lambda b, n: (row(b, n), 0)),
        out_shape=jax.ShapeDtypeStruct((bsz * seq, A_Q_COLS), BF16),
        compiler_params=_params(("parallel", "arbitrary")),
        name="window_attention",
    )(z_qkv, z_qkv, z_qkv, z_qkv, z_qkv, z_qkv, z_qkv, zm_qkv, zm_qkv, bias_w, bias_m)


def _mla_kernel(q_ref, k_ref, v_ref, km_ref, vm_ref, o_ref, acc_ref, sa_ref, sb_ref, pa_ref, pb_ref,
                *, tk, npair):
    last = 2 * npair - 1

    def rows(c):
        return pl.ds(pl.multiple_of(c * tk, tk), tk)

    def scores(c):
        return lax.dot_general(q_ref[...], k_ref[rows(c), :], _NT, preferred_element_type=F32)

    def weighted(p_ref, c):
        v1 = jnp.concatenate([v_ref[rows(c), :], jnp.ones((tk, V_HEAD), BF16)], axis=1)
        return jnp.dot(p_ref[...], v1, preferred_element_type=F32)

    def softmax(s_ref, p_ref, m):
        s = s_ref[...]
        m_new = jnp.maximum(m, jnp.max(s, axis=-1, keepdims=True))
        alpha = jnp.exp2(m - m_new)
        p_ref[...] = jnp.exp2((s - m_new).astype(BF16))
        return m_new, alpha

    s0 = lax.dot_general(q_ref[...], km_ref[...], _NT, preferred_element_type=F32)
    mcol = lax.broadcasted_iota(jnp.int32, (1, META_PAD), 1)
    s0 = jnp.where(mcol < N_META, s0, NEG)
    m0 = jnp.max(s0, axis=-1, keepdims=True)
    p0 = jnp.exp2((s0 - m0).astype(BF16))
    vm1 = jnp.concatenate([vm_ref[...], jnp.ones((META_PAD, V_HEAD), BF16)], axis=1)
    acc_ref[...] = jnp.dot(p0, vm1, preferred_element_type=F32)
    sa_ref[...] = scores(0)
    pb_ref[...] = jnp.zeros_like(pb_ref)

    def body(j, carry):
        m, alpha_b = carry
        c0 = 2 * j
        sb_ref[...] = scores(c0 + 1)
        acc_ref[...] = alpha_b * acc_ref[...] + weighted(pb_ref, jnp.maximum(c0 - 1, 0))
        m, alpha_a = softmax(sa_ref, pa_ref, m)
        sa_ref[...] = scores(jnp.minimum(c0 + 2, last))
        acc_ref[...] = alpha_a * acc_ref[...] + weighted(pa_ref, c0)
        m, alpha_b = softmax(sb_ref, pb_ref, m)
        return m, alpha_b

    _, alpha_b = lax.fori_loop(0, npair, body, (m0, jnp.ones_like(m0)))
    acc = alpha_b * acc_ref[...] + weighted(pb_ref, last)
    o_ref[...] = (acc[:, :V_HEAD] / acc[:, V_HEAD:]).astype(o_ref.dtype)


def mla_attention(q, k, v, km, vm, bsz, seq, tq_pref=1024, tk_pref=1024):
    tq, tk = _pick(seq, tq_pref), _pick(seq // 2, tk_pref)
    nq = seq // tq
    return pl.pallas_call(
        functools.partial(_mla_kernel, tk=tk, npair=seq // (2 * tk)),
        grid=(bsz, B_HEADS, nq),
        in_specs=[
            pl.BlockSpec((tq, QK_PAD), lambda b, h, i: (b * nq + i, h)),
            pl.BlockSpec((seq, QK_PAD), lambda b, h, i: (b, h)),
            pl.BlockSpec((seq, V_HEAD), lambda b, h, i: (b, h)),
            pl.BlockSpec((META_PAD, QK_PAD), lambda b, h, i: (0, h)),
            pl.BlockSpec((META_PAD, V_HEAD), lambda b, h, i: (0, h)),
        ],
        out_specs=pl.BlockSpec((tq, V_HEAD), lambda b, h, i: (b * nq + i, h)),
        out_shape=jax.ShapeDtypeStruct((bsz * seq, B_HEADS * V_HEAD), BF16),
        scratch_shapes=[pltpu.VMEM((tq, 2 * V_HEAD), F32),
                        pltpu.VMEM((tq, tk), F32), pltpu.VMEM((tq, tk), F32),
                        pltpu.VMEM((tq, tk), BF16), pltpu.VMEM((tq, tk), BF16)],
        compiler_params=_params(("parallel", "parallel", "arbitrary")),
        name="mla_attention",
    )(q, k, v, km, vm)


def _merge_kernel(ya_ref, yb_ref, wa_ref, wb_ref, ga_ref, gb_ref, o_ref):
    pa = jnp.dot(ya_ref[...], wa_ref[...], preferred_element_type=F32)
    pb = jnp.dot(yb_ref[...], wb_ref[...], preferred_element_type=F32)
    o_ref[...] = (jax.nn.sigmoid(ga_ref[...].astype(F32)) * pa
                  + jax.nn.sigmoid(gb_ref[...].astype(F32)) * pb).astype(o_ref.dtype)


def gated_merge(ya, yb, w_o_a, w_o_b, gates, tm_pref=1024, tn_pref=1024):
    t = ya.shape[0]
    tm, tn = _pick(t, tm_pref), _pick(D_MODEL, tn_pref)
    nj = D_MODEL // tn
    return pl.pallas_call(
        _merge_kernel,
        grid=(t // tm, nj),
        in_specs=[
            pl.BlockSpec((tm, A_Q_COLS), lambda i, j: (i, 0)),
            pl.BlockSpec((tm, B_HEADS * V_HEAD), lambda i, j: (i, 0)),
            pl.BlockSpec((A_Q_COLS, tn), lambda i, j: (0, j)),
            pl.BlockSpec((B_HEADS * V_HEAD, tn), lambda i, j: (0, j)),
            pl.BlockSpec((tm, tn), lambda i, j: (i, j)),
            pl.BlockSpec((tm, tn), lambda i, j: (i, j + nj)),
        ],
        out_specs=pl.BlockSpec((tm, tn), lambda i, j: (i, j)),
        out_shape=jax.ShapeDtypeStruct((t, D_MODEL), BF16),
        compiler_params=_params(("parallel", "arbitrary")),
        name="gated_merge",
    )(ya, yb, w_o_a, w_o_b, gates, gates)


def _out_proj_kernel(a_ref, w_ref, x_ref, o_ref):
    o_ref[...] = x_ref[...] + jnp.dot(a_ref[...], w_ref[...], preferred_element_type=F32)


def out_proj_residual(a, w, x, tm_pref=1024, tn_pref=1024):
    t, k = a.shape
    n = w.shape[1]
    tm, tn = _pick(t, tm_pref), _pick(n, tn_pref)
    return pl.pallas_call(
        _out_proj_kernel,
        grid=(t // tm, n // tn),
        in_specs=[
            pl.BlockSpec((tm, k), lambda i, j: (i, 0)),
            pl.BlockSpec((k, tn), lambda i, j: (0, j)),
            pl.BlockSpec((tm, tn), lambda i, j: (i, j)),
        ],
        out_specs=pl.BlockSpec((tm, tn), lambda i, j: (i, j)),
        out_shape=jax.ShapeDtypeStruct((t, n), F32),
        compiler_params=_params(("parallel", "arbitrary")),
        name="out_proj_residual",
    )(a, w, x)


def _top_values(s, k):
    rows = []
    for _ in range(k):
        m = jnp.max(s, axis=0, keepdims=True)
        rows.append(m)
        s = jnp.where(s == m, -jnp.inf, s)
    return rows


def _peer_route_kernel(h_ref, g_ref, wq_ref, sk1_ref, sk2_ref,
                       hf_ref, th1_ref, r1_ref, s2_ref, e2_ref, qt_ref):
    hf = _rms(h_ref[...], g_ref[...]).astype(BF16)
    hf_ref[...] = hf
    qt_ref[...] = lax.dot_general(wq_ref[...], hf, _NT, preferred_element_type=F32).astype(BF16)
    half = PEER_KEY_DIM // 2
    tc = hf.shape[0]
    row16 = lax.broadcasted_iota(jnp.int32, (PEER_TOPK, tc), 0)
    row8 = lax.broadcasted_iota(jnp.int32, (SUBLANES, tc), 0)

    def head(h):
        base = pl.multiple_of(h * PEER_KEY_DIM, PEER_KEY_DIM)
        q1 = qt_ref[pl.ds(base, half), :]
        q2 = qt_ref[pl.ds(base + half, half), :]
        s1 = jnp.dot(sk1_ref[...], q1, preferred_element_type=F32)
        s2 = jnp.dot(sk2_ref[...], q2, preferred_element_type=F32)
        v1 = _top_values(s1, PEER_TOPK + 1)
        v2 = _top_values(s2, PEER_TOPK + 1)
        v2_all = jnp.zeros((PEER_TOPK, tc), F32)
        for r in range(PEER_TOPK):
            v2_all = jnp.where(row16 == r, v2[r], v2_all)
        v2_top = jnp.zeros((SUBLANES, tc), F32)
        v1_low = jnp.zeros((SUBLANES, tc), F32)
        for r in range(SUBLANES):
            v2_top = jnp.where(row8 == r, v2[r], v2_top)
            v1_low = jnp.where(row8 == r, v1[SUBLANES + r], v1_low)
        blocks = [v1[0] + v2_all, v1[1] + v2_top]
        for a in range(3, SUBLANES + 1):
            blocks.append(jnp.where(row8 < PEER_TOPK // a, v1[a - 1] + v2_top, -jnp.inf))
        blocks.append(v1_low + v2[0])
        cand = jnp.concatenate(blocks, axis=0)
        rem = cand
        for _ in range(PEER_TOPK):
            tau = jnp.max(rem, axis=0, keepdims=True)
            rem = jnp.where(rem == tau, -jnp.inf, rem)
        below = jnp.maximum(jnp.max(rem, axis=0, keepdims=True),
                            jnp.maximum(v1[PEER_TOPK] + v2[0], v1[0] + v2[PEER_TOPK]))
        tau = 0.5 * (tau + below)
        m1, m2 = v1[0], v2[0]
        z = jnp.sum(jnp.where(cand >= tau, jnp.exp(cand - (m1 + m2)), 0.0), axis=0, keepdims=True)
        th1_ref[h] = tau - s1
        r1_ref[h] = jnp.exp(s1 - m1) / z
        s2_ref[h] = s2
        e2_ref[h] = jnp.exp(s2 - m2)

    def head_pair(j, carry):
        head(2 * j)
        head(2 * j + 1)
        return carry

    lax.fori_loop(0, PEER_HEADS // 2, head_pair, 0)


def peer_route(h1, norm_ffn, wq_t, sk1, sk2, tc_pref=256):
    t = h1.shape[0]
    tc = _pick(t, tc_pref)
    full = lambda a: pl.BlockSpec(a.shape, lambda i: (0,) * a.ndim)
    g = norm_ffn.reshape(1, -1).astype(F32)
    keyed = pl.BlockSpec((PEER_HEADS, N_KEYS, tc), lambda i: (0, 0, i))
    keyed_shape = jax.ShapeDtypeStruct((PEER_HEADS, N_KEYS, t), F32)
    return pl.pallas_call(
        _peer_route_kernel,
        grid=(t // tc,),
        in_specs=[pl.BlockSpec((tc, D_MODEL), lambda i: (i, 0)), full(g), full(wq_t), full(sk1), full(sk2)],
        out_specs=[pl.BlockSpec((tc, D_MODEL), lambda i: (i, 0)), keyed, keyed, keyed, keyed],
        out_shape=[jax.ShapeDtypeStruct((t, D_MODEL), BF16), keyed_shape, keyed_shape, keyed_shape,
                   keyed_shape],
        scratch_shapes=[pltpu.VMEM((PEER_HEADS * PEER_KEY_DIM, tc), BF16)],
        compiler_params=_params(("parallel",)),
        name="peer_route",
    )(h1, g, wq_t, sk1, sk2)


GATE_ROWS = 32
GATE_PIECES = 8


def _peer_expert_kernel(hf_ref, dn_ref, upt_ref, th1_ref, r1_ref, th1n_ref, r1n_ref, s2_ref, e2_ref,
                        h1_ref, gf_ref, o_ref, acc_ref, ga_ref, gb_ref, wt_ref, bth_ref, br_ref,
                        *, ib, ne):
    e = pl.program_id(1)
    tc = hf_ref.shape[0]
    eb = dn_ref.shape[0]
    strip = GATE_ROWS // SUBLANES

    def gate_rows(th_ref, r_ref, g_ref, il):
        for h in range(PEER_HEADS):
            bth_ref[h, il] = jnp.broadcast_to(th_ref[h, il:il + 1, :], (SUBLANES, tc))
            br_ref[h, il] = jnp.broadcast_to(r_ref[h, il:il + 1, :], (SUBLANES, tc))
        for jr in range(N_KEYS // GATE_ROWS):
            js = slice(jr * strip, (jr + 1) * strip)
            g = jnp.zeros((strip, SUBLANES, tc), F32)
            for h in range(PEER_HEADS):
                sel = s2_ref[h, js] >= bth_ref[h, il][None]
                g = g + jnp.where(sel, e2_ref[h, js], 0.0) * br_ref[h, il][None]
            base = il * (N_KEYS // SUBLANES) + jr * strip
            g_ref[base:base + strip] = g

    @pl.when(e == 0)
    def _():
        acc_ref[...] = jnp.zeros_like(acc_ref)

    @pl.when((e == 0) & (pl.program_id(0) == 0))
    def _():
        for il in range(ib):
            gate_rows(th1_ref, r1_ref, ga_ref, il)

    def block(g_cur_ref, g_next_ref):
        at = lax.dot_general(dn_ref[...], hf_ref[...], _NT, preferred_element_type=F32)
        wt_ref[...] = (0.5 * at * (1.0 + lax.erf(at * (2.0 ** -0.5)))
                       * g_cur_ref[...].reshape(eb, tc)).astype(BF16)
        rows, per = D_MODEL // GATE_PIECES, ib // GATE_PIECES
        for p in range(GATE_PIECES):
            rs = slice(p * rows, (p + 1) * rows)
            acc_ref[rs, :] += jnp.dot(upt_ref[rs, :], wt_ref[...], preferred_element_type=F32)
            for il in range(p * per, (p + 1) * per):
                gate_rows(th1n_ref, r1n_ref, g_next_ref, il)

    @pl.when(e % 2 == 0)
    def _():
        block(ga_ref, gb_ref)

    @pl.when(e % 2 == 1)
    def _():
        block(gb_ref, ga_ref)

    @pl.when(e == ne - 1)
    def _():
        o_ref[...] = _rms(h1_ref[...] + acc_ref[...].T, gf_ref[...])


def peer_experts(hf, down, up_t, th1, r1, s2, e2, h1, g_final, tc_pref=512, eb=1024):
    t = hf.shape[0]
    tc = _pick(t, tc_pref)
    ib = eb // N_KEYS
    ne = N_EXPERTS // eb
    assert ne % 2 == 0 and ib % GATE_PIECES == 0
    gf = g_final.reshape(1, -1).astype(F32)
    blk_i = pl.BlockSpec((PEER_HEADS, ib, tc), lambda c, e: (0, e, c))
    nc = t // tc
    tile_n = lambda c, e: jnp.minimum(c + (e + 1) // ne, nc - 1)
    blk_n = pl.BlockSpec((PEER_HEADS, ib, tc), lambda c, e: (0, (e + 1) % ne, tile_n(c, e)))
    s2, e2 = (a.reshape(PEER_HEADS, N_KEYS // SUBLANES, SUBLANES, t) for a in (s2, e2))
    blk_j = pl.BlockSpec((PEER_HEADS, N_KEYS // SUBLANES, SUBLANES, tc),
                         lambda c, e: (0, 0, 0, tile_n(c, e)))
    once = pl.Buffered(1)
    return pl.pallas_call(
        functools.partial(_peer_expert_kernel, ib=ib, ne=ne),
        grid=(t // tc, ne),
        in_specs=[
            pl.BlockSpec((tc, D_MODEL), lambda c, e: (c, 0), pipeline_mode=once),
            pl.BlockSpec((eb, D_MODEL), lambda c, e: (e, 0)),
            pl.BlockSpec((D_MODEL, eb), lambda c, e: (0, e)),
            blk_i, blk_i, blk_n, blk_n, blk_j, blk_j,
            pl.BlockSpec((tc, D_MODEL), lambda c, e: (c, 0), pipeline_mode=once),
            pl.BlockSpec((1, D_MODEL), lambda c, e: (0, 0)),
        ],
        out_specs=pl.BlockSpec((tc, D_MODEL), lambda c, e: (c, 0)),
        out_shape=jax.ShapeDtypeStruct((t, D_MODEL), F32),
        scratch_shapes=[pltpu.VMEM((D_MODEL, tc), F32),
                        pltpu.VMEM((eb // SUBLANES, SUBLANES, tc), F32),
                        pltpu.VMEM((eb // SUBLANES, SUBLANES, tc), F32),
                        pltpu.VMEM((eb, tc), BF16),
                        pltpu.VMEM((PEER_HEADS, ib, SUBLANES, tc), F32),
                        pltpu.VMEM((PEER_HEADS, ib, SUBLANES, tc), F32)],
        compiler_params=_params(("arbitrary", "arbitrary")),
        name="peer_experts",
    )(hf, down, up_t, th1, r1, th1, r1, s2, e2, h1, gf)


def _rel_bucket(rel):
    nb = REL_BUCKETS // 2
    max_exact = nb // 2
    ret = jnp.where(rel > 0, nb, 0)
    n = jnp.abs(rel)
    nf = jnp.maximum(n, 1).astype(F32)
    large = max_exact + (jnp.log(nf / max_exact) / math.log(REL_MAX_DIST / max_exact)
                         * (nb - max_exact)).astype(jnp.int32)
    large = jnp.minimum(large, nb - 1)
    return ret + jnp.where(n < max_exact, n, large)


def _window_bias(rel_table, sink):
    i = jnp.arange(BLOCK)
    c = jnp.arange(3 * BLOCK)
    rel_w = c[None, :] - BLOCK - i[:, None]
    bw = rel_table[_rel_bucket(rel_w)].astype(F32).transpose(2, 0, 1)
    bw = jnp.where((jnp.abs(rel_w) <= WINDOW)[None], bw, NEG)
    bw = bw.reshape(A_KV_HEADS, A_GROUP * BLOCK, 3 * BLOCK)
    q_pos = N_META + jnp.arange(2)[:, None] * BLOCK + i[None, :]
    rel_m = jnp.arange(N_META)[None, None, :] - q_pos[:, :, None]
    bm = rel_table[_rel_bucket(rel_m)].astype(F32).transpose(0, 3, 1, 2)
    sink_col = jnp.broadcast_to(sink.astype(F32)[None, :, None, None], (2, A_HEADS, BLOCK, 1))
    pad = jnp.full((2, A_HEADS, BLOCK, META_PAD - N_META - 1), NEG, F32)
    bm = jnp.concatenate([bm, sink_col, pad], axis=-1)
    return bw, bm.reshape(2, A_KV_HEADS, A_GROUP * BLOCK, META_PAD)


def _rope_tables(length):
    half = QK_ROPE // 2
    freqs = ROPE_THETA ** (-jnp.arange(half, dtype=F32) / half)
    ang = jnp.arange(length).astype(F32)[:, None] * freqs[None, :]
    c, s = jnp.cos(ang), jnp.sin(ang)
    z = jnp.zeros((length, LANES - QK_ROPE), F32)
    zh = jnp.zeros((length, half), F32)
    return (jnp.concatenate([c, c, z], axis=1),
            jnp.concatenate([-s, zh, z], axis=1),
            jnp.concatenate([zh, s, z], axis=1))


def _pad_rows(a, rows):
    return jnp.pad(a, ((0, rows - a.shape[0]), (0, 0)))


def kernel(x_prompt, x_sample, meta_tokens, rel_table, norm_mix, w_in, g_cq, w_uq, g_ckv, w_ukv, attn_sink,
           w_o_a, w_o_b, w_out, norm_ffn, w_query, sub_keys1, sub_keys2, expert_down, expert_up, g_final):
    assert w_in.shape[0] == 1, "single-layer block"
    wi = w_in[0]
    c1 = QKV_COLS
    c2 = c1 + Q_LORA + KV_LORA + QK_ROPE
    w_qkv = wi[:, :c1].astype(BF16)
    w_lat = jnp.pad(wi[:, c1:c2], ((0, 0), (0, LAT_COLS - (c2 - c1)))).astype(BF16)
    w_gate = wi[:, c2:].astype(BF16)
    wq = jnp.pad(w_uq[0].reshape(Q_LORA, B_HEADS, QK_NOPE + QK_ROPE),
                 ((0, 0), (0, 0), (0, QK_PAD - QK_NOPE - QK_ROPE))).reshape(Q_LORA, B_HEADS * QK_PAD).astype(BF16)
    wkv = w_ukv[0].reshape(KV_LORA, B_HEADS, QK_NOPE + V_HEAD)
    wk = wkv[:, :, :QK_NOPE].reshape(KV_LORA, B_HEADS * QK_NOPE).astype(BF16)
    wv = wkv[:, :, QK_NOPE:].reshape(KV_LORA, B_HEADS * V_HEAD).astype(BF16)
    woa, wob, wout = w_o_a[0].astype(BF16), w_o_b[0].astype(BF16), w_out[0].astype(BF16)
    wq_t = w_query[0].T.astype(BF16)
    sk1, sk2 = sub_keys1[0].astype(BF16), sub_keys2[0].astype(BF16)
    down, up_t = expert_down[0].astype(BF16), expert_up[0].astype(BF16).T
    bias_w, bias_m = _window_bias(rel_table, attn_sink[0])

    zm_qkv = norm_matmul(meta_tokens, norm_mix[0], w_qkv, BF16)
    zm_lat = norm_matmul(meta_tokens, norm_mix[0], w_lat, F32)
    _, km, vm = mla_proj(zm_lat, g_cq[0], g_ckv[0], wq, wk, wv, _rope_tables(N_META), N_META)
    zm_qkv, km, vm = _pad_rows(zm_qkv, META_PAD), _pad_rows(km, META_PAD), _pad_rows(vm, META_PAD)

    def encode(x):
        bsz, seq, _ = x.shape
        xt = x.reshape(bsz * seq, D_MODEL)
        tabs = tuple(t[N_META:] for t in _rope_tables(N_META + seq))
        z_qkv = norm_matmul(xt, norm_mix[0], w_qkv, BF16)
        z_lat = norm_matmul(xt, norm_mix[0], w_lat, F32)
        gates = norm_matmul(xt, norm_mix[0], w_gate, BF16)
        q, k, v = mla_proj(z_lat, g_cq[0], g_ckv[0], wq, wk, wv, tabs, seq)
        y_a = window_attention(z_qkv, zm_qkv, bias_w, bias_m, bsz, seq)
        y_b = mla_attention(q, k, v, km, vm, bsz, seq)
        merged = gated_merge(y_a, y_b, woa, wob, gates)
        h1 = out_proj_residual(merged, wout, xt)
        hf, th1, r1, s2, e2 = peer_route(h1, norm_ffn[0], wq_t, sk1, sk2)
        y = peer_experts(hf, down, up_t, th1, r1, s2, e2, h1, g_final)
        return y.reshape(bsz, seq, D_MODEL)

    return encode(x_prompt), encode(x_sample)
```

```python
import functools
import math

import jax
import jax.numpy as jnp
from jax import lax
from jax.experimental import pallas as pl
from jax.experimental.pallas import tpu as pltpu

F32 = jnp.float32
BF16 = jnp.bfloat16

D_MODEL = 2048
N_META = 16
BLOCK = 128
WINDOW = 128
A_HEADS = 16
A_KV_HEADS = 4
A_HEAD_DIM = 128
A_GROUP = A_HEADS // A_KV_HEADS
B_HEADS = 16
Q_LORA = 512
KV_LORA = 256
QK_NOPE = 128
QK_ROPE = 64
V_HEAD = 128
ROPE_THETA = 10000.0
REL_BUCKETS = 32
REL_MAX_DIST = 128
PEER_HEADS = 8
N_KEYS = 128
N_EXPERTS = N_KEYS * N_KEYS
PEER_KEY_DIM = 128
PEER_TOPK = 16
EPS = 1e-6
NEG = -1e30

A_Q_COLS = A_HEADS * A_HEAD_DIM
A_KV_COLS = A_KV_HEADS * A_HEAD_DIM
QKV_COLS = A_Q_COLS + 2 * A_KV_COLS
LAT_COLS = 896
QK_PAD = 256
LANES = 128
SUBLANES = 8
META_PAD = 128
VMEM_LIMIT = 56 * 1024 * 1024

_NT = (((1,), (1,)), ((), ()))
_TN = (((0,), (0,)), ((), ()))


def _params(sem):
    return pltpu.CompilerParams(dimension_semantics=sem, vmem_limit_bytes=VMEM_LIMIT)


def _pick(n, pref):
    if n <= pref:
        return n
    t = pref
    while n % t:
        t //= 2
    return t


def _rms(xf, g):
    ms = jnp.mean(xf * xf, axis=-1, keepdims=True)
    return xf * lax.rsqrt(ms + EPS) * g


def _norm_matmul_kernel(x_ref, g_ref, w_ref, o_ref, xn_ref):
    @pl.when(pl.program_id(1) == 0)
    def _():
        xn_ref[...] = _rms(x_ref[...].astype(F32), g_ref[...]).astype(BF16)

    o_ref[...] = jnp.dot(xn_ref[...], w_ref[...], preferred_element_type=F32).astype(o_ref.dtype)


def norm_matmul(x, g, w, out_dtype, tm_pref=1024, tn_pref=1024):
    m, k = x.shape
    n = w.shape[1]
    tm, tn = _pick(m, tm_pref), _pick(n, tn_pref)
    return pl.pallas_call(
        _norm_matmul_kernel,
        grid=(m // tm, n // tn),
        in_specs=[
            pl.BlockSpec((tm, k), lambda i, j: (i, 0)),
            pl.BlockSpec((1, k), lambda i, j: (0, 0)),
            pl.BlockSpec((k, tn), lambda i, j: (0, j)),
        ],
        out_specs=pl.BlockSpec((tm, tn), lambda i, j: (i, j)),
        out_shape=jax.ShapeDtypeStruct((m, n), out_dtype),
        scratch_shapes=[pltpu.VMEM((tm, k), BF16)],
        compiler_params=_params(("parallel", "arbitrary")),
        name="norm_matmul",
    )(x, g.reshape(1, k).astype(F32), w)


def _rope128(xg, cos, sin_lo, sin_hi):
    return xg * cos + pltpu.roll(xg, 96, 1) * sin_lo + pltpu.roll(xg, 32, 1) * sin_hi


def _mla_proj_kernel(z_ref, gq_ref, gkv_ref, wq_ref, wk_ref, wv_ref, cos_ref, slo_ref, shi_ref,
                     q_ref, k_ref, v_ref, *, q_scale):
    z = z_ref[...]
    nq = _rms(z[:, :Q_LORA], gq_ref[...]).astype(BF16)
    nkv = _rms(z[:, Q_LORA:Q_LORA + KV_LORA], gkv_ref[...]).astype(BF16)
    cos, slo, shi = cos_ref[...], slo_ref[...], shi_ref[...]
    kr = _rope128(z[:, Q_LORA + KV_LORA:], cos, slo, shi).astype(BF16)
    for h in range(B_HEADS):
        qh = jnp.dot(nq, wq_ref[:, h * QK_PAD:(h + 1) * QK_PAD], preferred_element_type=F32)
        q_ref[:, h * QK_PAD:h * QK_PAD + LANES] = (qh[:, :LANES] * q_scale).astype(BF16)
        q_ref[:, h * QK_PAD + LANES:(h + 1) * QK_PAD] = (
            _rope128(qh[:, LANES:], cos, slo, shi) * q_scale).astype(BF16)
        kh = jnp.dot(nkv, wk_ref[:, h * QK_NOPE:(h + 1) * QK_NOPE], preferred_element_type=F32)
        k_ref[:, h * QK_PAD:h * QK_PAD + LANES] = kh.astype(BF16)
        k_ref[:, h * QK_PAD + LANES:(h + 1) * QK_PAD] = kr
        vh = jnp.dot(nkv, wv_ref[:, h * V_HEAD:(h + 1) * V_HEAD], preferred_element_type=F32)
        v_ref[:, h * V_HEAD:(h + 1) * V_HEAD] = vh.astype(BF16)


def mla_proj(z_lat, g_cq, g_ckv, wq, wk, wv, rope_tabs, seq, tm_pref=256):
    t = z_lat.shape[0]
    tm = _pick(min(t, seq), tm_pref)
    nseq = seq // tm
    q_scale = (QK_NOPE + QK_ROPE) ** -0.5 * math.log2(math.e)
    full = lambda a: pl.BlockSpec(a.shape, lambda i: (0, 0))
    tab = pl.BlockSpec((tm, LANES), lambda i: (i % nseq, 0))
    gq, gkv = g_cq.reshape(1, -1).astype(F32), g_ckv.reshape(1, -1).astype(F32)
    return pl.pallas_call(
        functools.partial(_mla_proj_kernel, q_scale=q_scale),
        grid=(t // tm,),
        in_specs=[pl.BlockSpec((tm, LAT_COLS), lambda i: (i, 0)), full(gq), full(gkv),
                  full(wq), full(wk), full(wv), tab, tab, tab],
        out_specs=[pl.BlockSpec((tm, B_HEADS * QK_PAD), lambda i: (i, 0)),
                   pl.BlockSpec((tm, B_HEADS * QK_PAD), lambda i: (i, 0)),
                   pl.BlockSpec((tm, B_HEADS * V_HEAD), lambda i: (i, 0))],
        out_shape=[jax.ShapeDtypeStruct((t, B_HEADS * QK_PAD), BF16),
                   jax.ShapeDtypeStruct((t, B_HEADS * QK_PAD), BF16),
                   jax.ShapeDtypeStruct((t, B_HEADS * V_HEAD), BF16)],
        compiler_params=_params(("parallel",)),
        name="mla_proj",
    )(z_lat, gq, gkv, wq, wk, wv, *rope_tabs)


def _window_kernel(q_ref, kl_ref, kc_ref, kr_ref, vl_ref, vc_ref, vr_ref, km_ref, vm_ref,
                   bw_ref, bm_ref, o_ref, *, nblk):
    n = pl.program_id(1)
    scale = A_HEAD_DIM ** -0.5
    col = lax.broadcasted_iota(jnp.int32, (1, 3 * BLOCK), 1)
    left_pen = jnp.where(n > 0, 0.0, NEG).astype(F32)
    right_pen = jnp.where(n < nblk - 1, 0.0, NEG).astype(F32)
    edge = jnp.where(col < BLOCK, left_pen, 0.0) + jnp.where(col >= 2 * BLOCK, right_pen, 0.0)
    for kh in range(A_KV_HEADS):
        cs = slice(kh * A_HEAD_DIM, (kh + 1) * A_HEAD_DIM)
        keys = jnp.concatenate([kl_ref[:, cs], kc_ref[:, cs], kr_ref[:, cs], km_ref[:, cs]], axis=0)
        vals = jnp.concatenate([vl_ref[:, cs], vc_ref[:, cs], vr_ref[:, cs], vm_ref[:, cs]], axis=0)
        q4 = jnp.concatenate(
            [q_ref[:, (kh * A_GROUP + g) * A_HEAD_DIM:(kh * A_GROUP + g + 1) * A_HEAD_DIM]
             for g in range(A_GROUP)], axis=0)
        bias = jnp.concatenate([bw_ref[kh] + edge, bm_ref[0, kh]], axis=1)
        s = lax.dot_general(q4, keys, _NT, preferred_element_type=F32) * scale + bias
        p = jnp.exp(s - jnp.max(s, axis=-1, keepdims=True))
        o = jnp.dot(p.astype(BF16), vals, preferred_element_type=F32) / jnp.sum(p, axis=-1, keepdims=True)
        for g in range(A_GROUP):
            h = kh * A_GROUP + g
            o_ref[:, h * A_HEAD_DIM:(h + 1) * A_HEAD_DIM] = o[g * BLOCK:(g + 1) * BLOCK].astype(o_ref.dtype)


def window_attention(z_qkv, zm_qkv, bias_w, bias_m, bsz, seq):
    nblk = seq // BLOCK
    kcol, vcol = A_Q_COLS // A_KV_COLS, A_Q_COLS // A_KV_COLS + 1
    row = lambda b, n: b * nblk + n
    lo = lambda b, n: b * nblk + jnp.maximum(n - 1, 0)
    hi = lambda b, n: b * nblk + jnp.minimum(n + 1, nblk - 1)
    kv = lambda r, c: pl.BlockSpec((BLOCK, A_KV_COLS), lambda b, n: (r(b, n), c))
    return pl.pallas_call(
        functools.partial(_window_kernel, nblk=nblk),
        grid=(bsz, nblk),
        in_specs=[
            pl.BlockSpec((BLOCK, A_Q_COLS), lambda b, n: (row(b, n), 0)),
            kv(lo, kcol), kv(row, kcol), kv(hi, kcol),
            kv(lo, vcol), kv(row, vcol), kv(hi, vcol),
            pl.BlockSpec((META_PAD, A_KV_COLS), lambda b, n: (0, kcol)),
            pl.BlockSpec((META_PAD, A_KV_COLS), lambda b, n: (0, vcol)),
            pl.BlockSpec(bias_w.shape, lambda b, n: (0, 0, 0)),
            pl.BlockSpec((1,) + bias_m.shape[1:], lambda b, n: (jnp.minimum(n, 1), 0, 0, 0)),
        ],
        out_specs=pl.BlockSpec((BLOCK, A_Q_COLS), lambda b, n: (row(b, n), 0)),
        out_shape=jax.ShapeDtypeStruct((bsz * seq, A_Q_COLS), BF16),
        compiler_params=_params(("parallel", "arbitrary")),
        name="window_attention",
    )(z_qkv, z_qkv, z_qkv, z_qkv, z_qkv, z_qkv, z_qkv, zm_qkv, zm_qkv, bias_w, bias_m)


def _mla_kernel(q_ref, k_ref, v_ref, km_ref, vm_ref, o_ref, acc_ref, sa_ref, sb_ref, pa_ref, pb_ref,
                *, tk, npair):
    last = 2 * npair - 1

    def rows(c):
        return pl.ds(pl.multiple_of(c * tk, tk), tk)

    def scores(c):
        return lax.dot_general(q_ref[...], k_ref[rows(c), :], _NT, preferred_element_type=F32)

    def weighted(p_ref, c):
        v1 = jnp.concatenate([v_ref[rows(c), :], jnp.ones((tk, V_HEAD), BF16)], axis=1)
        return jnp.dot(p_ref[...], v1, preferred_element_type=F32)

    def softmax(s_ref, p_ref, m):
        s = s_ref[...]
        m_new = jnp.maximum(m, jnp.max(s, axis=-1, keepdims=True))
        alpha = jnp.exp2(m - m_new)
        p_ref[...] = jnp.exp2((s - m_new).astype(BF16))
        return m_new, alpha

    s0 = lax.dot_general(q_ref[...], km_ref[...], _NT, preferred_element_type=F32)
    mcol = lax.broadcasted_iota(jnp.int32, (1, META_PAD), 1)
    s0 = jnp.where(mcol < N_META, s0, NEG)
    m0 = jnp.max(s0, axis=-1, keepdims=True)
    p0 = jnp.exp2((s0 - m0).astype(BF16))
    vm1 = jnp.concatenate([vm_ref[...], jnp.ones((META_PAD, V_HEAD), BF16)], axis=1)
    acc_ref[...] = jnp.dot(p0, vm1, preferred_element_type=F32)
    sa_ref[...] = scores(0)
    pb_ref[...] = jnp.zeros_like(pb_ref)

    def body(j, carry):
        m, alpha_b = carry
        c0 = 2 * j
        sb_ref[...] = scores(c0 + 1)
        acc_ref[...] = alpha_b * acc_ref[...] + weighted(pb_ref, jnp.maximum(c0 - 1, 0))
        m, alpha_a = softmax(sa_ref, pa_ref, m)
        sa_ref[...] = scores(jnp.minimum(c0 + 2, last))
        acc_ref[...] = alpha_a * acc_ref[...] + weighted(pa_ref, c0)
        m, alpha_b = softmax(sb_ref, pb_ref, m)
        return m, alpha_b

    _, alpha_b = lax.fori_loop(0, npair, body, (m0, jnp.ones_like(m0)))
    acc = alpha_b * acc_ref[...] + weighted(pb_ref, last)
    o_ref[...] = (acc[:, :V_HEAD] / acc[:, V_HEAD:]).astype(o_ref.dtype)


def mla_attention(q, k, v, km, vm, bsz, seq, tq_pref=1024, tk_pref=1024):
    tq, tk = _pick(seq, tq_pref), _pick(seq // 2, tk_pref)
    nq = seq // tq
    return pl.pallas_call(
        functools.partial(_mla_kernel, tk=tk, npair=seq // (2 * tk)),
        grid=(bsz, B_HEADS, nq),
        in_specs=[
            pl.BlockSpec((tq, QK_PAD), lambda b, h, i: (b * nq + i, h)),
            pl.BlockSpec((seq, QK_PAD), lambda b, h, i: (b, h)),
            pl.BlockSpec((seq, V_HEAD), lambda b, h, i: (b, h)),
            pl.BlockSpec((META_PAD, QK_PAD), lambda b, h, i: (0, h)),
            pl.BlockSpec((META_PAD, V_HEAD), lambda b, h, i: (0, h)),
        ],
        out_specs=pl.BlockSpec((tq, V_HEAD), lambda b, h, i: (b * nq + i, h)),
        out_shape=jax.ShapeDtypeStruct((bsz * seq, B_HEADS * V_HEAD), BF16),
        scratch_shapes=[pltpu.VMEM((tq, 2 * V_HEAD), F32),
                        pltpu.VMEM((tq, tk), F32), pltpu.VMEM((tq, tk), F32),
                        pltpu.VMEM((tq, tk), BF16), pltpu.VMEM((tq, tk), BF16)],
        compiler_params=_params(("parallel", "parallel", "arbitrary")),
        name="mla_attention",
    )(q, k, v, km, vm)


def _merge_out_kernel(ya_ref, yb_ref, wa_ref, wb_ref, ga_ref, gb_ref, wo_ref, x_ref, o_ref, m_ref, *, nj):
    j = pl.program_id(1)

    @pl.when(j < nj)
    def _():
        pa = jnp.dot(ya_ref[...], wa_ref[...], preferred_element_type=F32)
        pb = jnp.dot(yb_ref[...], wb_ref[...], preferred_element_type=F32)
        m_ref[j] = (jax.nn.sigmoid(ga_ref[...].astype(F32)) * pa
                    + jax.nn.sigmoid(gb_ref[...].astype(F32)) * pb).astype(BF16)

    @pl.when(j >= nj)
    def _():
        merged = jnp.concatenate([m_ref[k] for k in range(nj)], axis=1)
        o_ref[...] = x_ref[...] + jnp.dot(merged, wo_ref[...], preferred_element_type=F32)


def merge_out_proj(ya, yb, w_o_a, w_o_b, gates, w_out, x, tm_pref=1024, tn_pref=512):
    t = ya.shape[0]
    tm, tn = _pick(t, tm_pref), _pick(D_MODEL, tn_pref)
    nj = D_MODEL // tn
    first = lambda j: jnp.minimum(j, nj - 1)
    second = lambda j: jnp.maximum(j - nj, 0)
    return pl.pallas_call(
        functools.partial(_merge_out_kernel, nj=nj),
        grid=(t // tm, 2 * nj),
        in_specs=[
            pl.BlockSpec((tm, A_Q_COLS), lambda i, j: (i, 0)),
            pl.BlockSpec((tm, B_HEADS * V_HEAD), lambda i, j: (i, 0)),
            pl.BlockSpec((A_Q_COLS, tn), lambda i, j: (0, first(j))),
            pl.BlockSpec((B_HEADS * V_HEAD, tn), lambda i, j: (0, first(j))),
            pl.BlockSpec((tm, tn), lambda i, j: (i, first(j))),
            pl.BlockSpec((tm, tn), lambda i, j: (i, first(j) + nj)),
            pl.BlockSpec((D_MODEL, tn), lambda i, j: (0, second(j))),
            pl.BlockSpec((tm, tn), lambda i, j: (i, second(j))),
        ],
        out_specs=pl.BlockSpec((tm, tn), lambda i, j: (i, second(j))),
        out_shape=jax.ShapeDtypeStruct((t, D_MODEL), F32),
        scratch_shapes=[pltpu.VMEM((nj, tm, tn), BF16)],
        compiler_params=_params(("parallel", "arbitrary")),
        name="merge_out_proj",
    )(ya, yb, w_o_a, w_o_b, gates, gates, w_out, x)


def _merge_kernel(ya_ref, yb_ref, wa_ref, wb_ref, ga_ref, gb_ref, o_ref):
    pa = jnp.dot(ya_ref[...], wa_ref[...], preferred_element_type=F32)
    pb = jnp.dot(yb_ref[...], wb_ref[...], preferred_element_type=F32)
    o_ref[...] = (jax.nn.sigmoid(ga_ref[...].astype(F32)) * pa
                  + jax.nn.sigmoid(gb_ref[...].astype(F32)) * pb).astype(o_ref.dtype)


def gated_merge(ya, yb, w_o_a, w_o_b, gates, tm_pref=1024, tn_pref=1024):
    t = ya.shape[0]
    tm, tn = _pick(t, tm_pref), _pick(D_MODEL, tn_pref)
    nj = D_MODEL // tn
    return pl.pallas_call(
        _merge_kernel,
        grid=(t // tm, nj),
        in_specs=[
            pl.BlockSpec((tm, A_Q_COLS), lambda i, j: (i, 0)),
            pl.BlockSpec((tm, B_HEADS * V_HEAD), lambda i, j: (i, 0)),
            pl.BlockSpec((A_Q_COLS, tn), lambda i, j: (0, j)),
            pl.BlockSpec((B_HEADS * V_HEAD, tn), lambda i, j: (0, j)),
            pl.BlockSpec((tm, tn), lambda i, j: (i, j)),
            pl.BlockSpec((tm, tn), lambda i, j: (i, j + nj)),
        ],
        out_specs=pl.BlockSpec((tm, tn), lambda i, j: (i, j)),
        out_shape=jax.ShapeDtypeStruct((t, D_MODEL), BF16),
        compiler_params=_params(("parallel", "arbitrary")),
        name="gated_merge",
    )(ya, yb, w_o_a, w_o_b, gates, gates)


def _out_proj_kernel(a_ref, w_ref, x_ref, o_ref):
    o_ref[...] = x_ref[...] + jnp.dot(a_ref[...], w_ref[...], preferred_element_type=F32)


def out_proj_residual(a, w, x, tm_pref=1024, tn_pref=1024):
    t, k = a.shape
    n = w.shape[1]
    tm, tn = _pick(t, tm_pref), _pick(n, tn_pref)
    return pl.pallas_call(
        _out_proj_kernel,
        grid=(t // tm, n // tn),
        in_specs=[
            pl.BlockSpec((tm, k), lambda i, j: (i, 0)),
            pl.BlockSpec((k, tn), lambda i, j: (0, j)),
            pl.BlockSpec((tm, tn), lambda i, j: (i, j)),
        ],
        out_specs=pl.BlockSpec((tm, tn), lambda i, j: (i, j)),
        out_shape=jax.ShapeDtypeStruct((t, n), F32),
        compiler_params=_params(("parallel", "arbitrary")),
        name="out_proj_residual",
    )(a, w, x)


def _top_values(s, k):
    rows = []
    for _ in range(k):
        m = jnp.max(s, axis=0, keepdims=True)
        rows.append(m)
        s = jnp.where(s == m, -jnp.inf, s)
    return rows


def _peer_route_kernel(h_ref, g_ref, wq_ref, sk1_ref, sk2_ref,
                       hf_ref, th1_ref, r1_ref, s2_ref, e2_ref, qt_ref):
    hf = _rms(h_ref[...], g_ref[...]).astype(BF16)
    hf_ref[...] = hf
    qt_ref[...] = lax.dot_general(wq_ref[...], hf, _NT, preferred_element_type=F32).astype(BF16)
    half = PEER_KEY_DIM // 2
    tc = hf.shape[0]
    row16 = lax.broadcasted_iota(jnp.int32, (PEER_TOPK, tc), 0)
    row8 = lax.broadcasted_iota(jnp.int32, (SUBLANES, tc), 0)

    def head(h):
        base = pl.multiple_of(h * PEER_KEY_DIM, PEER_KEY_DIM)
        q1 = qt_ref[pl.ds(base, half), :]
        q2 = qt_ref[pl.ds(base + half, half), :]
        s1 = jnp.dot(sk1_ref[...], q1, preferred_element_type=F32)
        s2 = jnp.dot(sk2_ref[...], q2, preferred_element_type=F32)
        v1 = _top_values(s1, PEER_TOPK + 1)
        v2 = _top_values(s2, PEER_TOPK + 1)
        v2_all = jnp.zeros((PEER_TOPK, tc), F32)
        for r in range(PEER_TOPK):
            v2_all = jnp.where(row16 == r, v2[r], v2_all)
        v2_top = jnp.zeros((SUBLANES, tc), F32)
        v1_low = jnp.zeros((SUBLANES, tc), F32)
        for r in range(SUBLANES):
            v2_top = jnp.where(row8 == r, v2[r], v2_top)
            v1_low = jnp.where(row8 == r, v1[SUBLANES + r], v1_low)
        blocks = [v1[0] + v2_all, v1[1] + v2_top]
        for a in range(3, SUBLANES + 1):
            blocks.append(jnp.where(row8 < PEER_TOPK // a, v1[a - 1] + v2_top, -jnp.inf))
        blocks.append(v1_low + v2[0])
        cand = jnp.concatenate(blocks, axis=0)
        rem = cand
        for _ in range(PEER_TOPK):
            tau = jnp.max(rem, axis=0, keepdims=True)
            rem = jnp.where(rem == tau, -jnp.inf, rem)
        below = jnp.maximum(jnp.max(rem, axis=0, keepdims=True),
                            jnp.maximum(v1[PEER_TOPK] + v2[0], v1[0] + v2[PEER_TOPK]))
        tau = 0.5 * (tau + below)
        m1, m2 = v1[0], v2[0]
        z = jnp.sum(jnp.where(cand >= tau, jnp.exp(cand - (m1 + m2)), 0.0), axis=0, keepdims=True)
        th1_ref[h] = tau - s1
        r1_ref[h] = jnp.exp(s1 - m1) / z
        s2_ref[h] = s2
        e2_ref[h] = jnp.exp(s2 - m2)

    def head_pair(j, carry):
        head(2 * j)
        head(2 * j + 1)
        return carry

    lax.fori_loop(0, PEER_HEADS // 2, head_pair, 0)


def peer_route(h1, norm_ffn, wq_t, sk1, sk2, tc_pref=256):
    t = h1.shape[0]
    tc = _pick(t, tc_pref)
    full = lambda a: pl.BlockSpec(a.shape, lambda i: (0,) * a.ndim)
    g = norm_ffn.reshape(1, -1).astype(F32)
    keyed = pl.BlockSpec((PEER_HEADS, N_KEYS, tc), lambda i: (0, 0, i))
    keyed_shape = jax.ShapeDtypeStruct((PEER_HEADS, N_KEYS, t), F32)
    return pl.pallas_call(
        _peer_route_kernel,
        grid=(t // tc,),
        in_specs=[pl.BlockSpec((tc, D_MODEL), lambda i: (i, 0)), full(g), full(wq_t), full(sk1), full(sk2)],
        out_specs=[pl.BlockSpec((tc, D_MODEL), lambda i: (i, 0)), keyed, keyed, keyed, keyed],
        out_shape=[jax.ShapeDtypeStruct((t, D_MODEL), BF16), keyed_shape, keyed_shape, keyed_shape,
                   keyed_shape],
        scratch_shapes=[pltpu.VMEM((PEER_HEADS * PEER_KEY_DIM, tc), BF16)],
        compiler_params=_params(("parallel",)),
        name="peer_route",
    )(h1, g, wq_t, sk1, sk2)


GATE_ROWS = 32
GATE_PIECES = 8


def _peer_expert_kernel(hf_ref, dn_ref, upt_ref, th1_ref, r1_ref, th1n_ref, r1n_ref, s2_ref, e2_ref,
                        h1_ref, gf_ref, o_ref, acc_ref, ga_ref, gb_ref, wt_ref, bth_ref, br_ref,
                        *, ib, ne):
    e = pl.program_id(1)
    tc = hf_ref.shape[0]
    eb = dn_ref.shape[0]
    strip = GATE_ROWS // SUBLANES

    def gate_rows(th_ref, r_ref, g_ref, il):
        for h in range(PEER_HEADS):
            bth_ref[h, il] = jnp.broadcast_to(th_ref[h, il:il + 1, :], (SUBLANES, tc))
            br_ref[h, il] = jnp.broadcast_to(r_ref[h, il:il + 1, :], (SUBLANES, tc))
        for jr in range(N_KEYS // GATE_ROWS):
            js = slice(jr * strip, (jr + 1) * strip)
            g = jnp.zeros((strip, SUBLANES, tc), F32)
            for h in range(PEER_HEADS):
                sel = s2_ref[h, js] >= bth_ref[h, il][None]
                g = g + jnp.where(sel, e2_ref[h, js], 0.0) * br_ref[h, il][None]
            base = il * (N_KEYS // SUBLANES) + jr * strip
            g_ref[base:base + strip] = g

    @pl.when(e == 0)
    def _():
        acc_ref[...] = jnp.zeros_like(acc_ref)

    @pl.when((e == 0) & (pl.program_id(0) == 0))
    def _():
        for il in range(ib):
            gate_rows(th1_ref, r1_ref, ga_ref, il)

    def block(g_cur_ref, g_next_ref):
        at = lax.dot_general(dn_ref[...], hf_ref[...], _NT, preferred_element_type=F32)
        wt_ref[...] = (0.5 * at * (1.0 + lax.erf(at * (2.0 ** -0.5)))
                       * g_cur_ref[...].reshape(eb, tc)).astype(BF16)
        rows, per = D_MODEL // GATE_PIECES, ib // GATE_PIECES
        for p in range(GATE_PIECES):
            rs = slice(p * rows, (p + 1) * rows)
            acc_ref[rs, :] += jnp.dot(upt_ref[rs, :], wt_ref[...], preferred_element_type=F32)
            for il in range(p * per, (p + 1) * per):
                gate_rows(th1n_ref, r1n_ref, g_next_ref, il)

    @pl.when(e % 2 == 0)
    def _():
        block(ga_ref, gb_ref)

    @pl.when(e % 2 == 1)
    def _():
        block(gb_ref, ga_ref)

    @pl.when(e == ne - 1)
    def _():
        o_ref[...] = _rms(h1_ref[...] + acc_ref[...].T, gf_ref[...])


def peer_experts(hf, down, up_t, th1, r1, s2, e2, h1, g_final, tc_pref=512, eb=1024):
    t = hf.shape[0]
    tc = _pick(t, tc_pref)
    ib = eb // N_KEYS
    ne = N_EXPERTS // eb
    assert ne % 2 == 0 and ib % GATE_PIECES == 0
    gf = g_final.reshape(1, -1).astype(F32)
    blk_i = pl.BlockSpec((PEER_HEADS, ib, tc), lambda c, e: (0, e, c))
    nc = t // tc
    tile_n = lambda c, e: jnp.minimum(c + (e + 1) // ne, nc - 1)
    blk_n = pl.BlockSpec((PEER_HEADS, ib, tc), lambda c, e: (0, (e + 1) % ne, tile_n(c, e)))
    s2, e2 = (a.reshape(PEER_HEADS, N_KEYS // SUBLANES, SUBLANES, t) for a in (s2, e2))
    blk_j = pl.BlockSpec((PEER_HEADS, N_KEYS // SUBLANES, SUBLANES, tc),
                         lambda c, e: (0, 0, 0, tile_n(c, e)))
    once = pl.Buffered(1)
    return pl.pallas_call(
        functools.partial(_peer_expert_kernel, ib=ib, ne=ne),
        grid=(t // tc, ne),
        in_specs=[
            pl.BlockSpec((tc, D_MODEL), lambda c, e: (c, 0), pipeline_mode=once),
            pl.BlockSpec((eb, D_MODEL), lambda c, e: (e, 0)),
            pl.BlockSpec((D_MODEL, eb), lambda c, e: (0, e)),
            blk_i, blk_i, blk_n, blk_n, blk_j, blk_j,
            pl.BlockSpec((tc, D_MODEL), lambda c, e: (c, 0), pipeline_mode=once),
            pl.BlockSpec((1, D_MODEL), lambda c, e: (0, 0)),
        ],
        out_specs=pl.BlockSpec((tc, D_MODEL), lambda c, e: (c, 0)),
        out_shape=jax.ShapeDtypeStruct((t, D_MODEL), F32),
        scratch_shapes=[pltpu.VMEM((D_MODEL, tc), F32),
                        pltpu.VMEM((eb // SUBLANES, SUBLANES, tc), F32),
                        pltpu.VMEM((eb // SUBLANES, SUBLANES, tc), F32),
                        pltpu.VMEM((eb, tc), BF16),
                        pltpu.VMEM((PEER_HEADS, ib, SUBLANES, tc), F32),
                        pltpu.VMEM((PEER_HEADS, ib, SUBLANES, tc), F32)],
        compiler_params=_params(("arbitrary", "arbitrary")),
        name="peer_experts",
    )(hf, down, up_t, th1, r1, th1, r1, s2, e2, h1, gf)


def _rel_bucket(rel):
    nb = REL_BUCKETS // 2
    max_exact = nb // 2
    ret = jnp.where(rel > 0, nb, 0)
    n = jnp.abs(rel)
    nf = jnp.maximum(n, 1).astype(F32)
    large = max_exact + (jnp.log(nf / max_exact) / math.log(REL_MAX_DIST / max_exact)
                         * (nb - max_exact)).astype(jnp.int32)
    large = jnp.minimum(large, nb - 1)
    return ret + jnp.where(n < max_exact, n, large)


def _window_bias(rel_table, sink):
    i = jnp.arange(BLOCK)
    c = jnp.arange(3 * BLOCK)
    rel_w = c[None, :] - BLOCK - i[:, None]
    bw = rel_table[_rel_bucket(rel_w)].astype(F32).transpose(2, 0, 1)
    bw = jnp.where((jnp.abs(rel_w) <= WINDOW)[None], bw, NEG)
    bw = bw.reshape(A_KV_HEADS, A_GROUP * BLOCK, 3 * BLOCK)
    q_pos = N_META + jnp.arange(2)[:, None] * BLOCK + i[None, :]
    rel_m = jnp.arange(N_META)[None, None, :] - q_pos[:, :, None]
    bm = rel_table[_rel_bucket(rel_m)].astype(F32).transpose(0, 3, 1, 2)
    sink_col = jnp.broadcast_to(sink.astype(F32)[None, :, None, None], (2, A_HEADS, BLOCK, 1))
    pad = jnp.full((2, A_HEADS, BLOCK, META_PAD - N_META - 1), NEG, F32)
    bm = jnp.concatenate([bm, sink_col, pad], axis=-1)
    return bw, bm.reshape(2, A_KV_HEADS, A_GROUP * BLOCK, META_PAD)


def _rope_tables(length):
    half = QK_ROPE // 2
    freqs = ROPE_THETA ** (-jnp.arange(half, dtype=F32) / half)
    ang = jnp.arange(length).astype(F32)[:, None] * freqs[None, :]
    c, s = jnp.cos(ang), jnp.sin(ang)
    z = jnp.zeros((length, LANES - QK_ROPE), F32)
    zh = jnp.zeros((length, half), F32)
    return (jnp.concatenate([c, c, z], axis=1),
            jnp.concatenate([-s, zh, z], axis=1),
            jnp.concatenate([zh, s, z], axis=1))


def _pad_rows(a, rows):
    return jnp.pad(a, ((0, rows - a.shape[0]), (0, 0)))


def kernel(x_prompt, x_sample, meta_tokens, rel_table, norm_mix, w_in, g_cq, w_uq, g_ckv, w_ukv, attn_sink,
           w_o_a, w_o_b, w_out, norm_ffn, w_query, sub_keys1, sub_keys2, expert_down, expert_up, g_final):
    assert w_in.shape[0] == 1, "single-layer block"
    wi = w_in[0]
    c1 = QKV_COLS
    c2 = c1 + Q_LORA + KV_LORA + QK_ROPE
    w_qkv = wi[:, :c1].astype(BF16)
    w_lat = jnp.pad(wi[:, c1:c2], ((0, 0), (0, LAT_COLS - (c2 - c1)))).astype(BF16)
    w_gate = wi[:, c2:].astype(BF16)
    wq = jnp.pad(w_uq[0].reshape(Q_LORA, B_HEADS, QK_NOPE + QK_ROPE),
                 ((0, 0), (0, 0), (0, QK_PAD - QK_NOPE - QK_ROPE))).reshape(Q_LORA, B_HEADS * QK_PAD).astype(BF16)
    wkv = w_ukv[0].reshape(KV_LORA, B_HEADS, QK_NOPE + V_HEAD)
    wk = wkv[:, :, :QK_NOPE].reshape(KV_LORA, B_HEADS * QK_NOPE).astype(BF16)
    wv = wkv[:, :, QK_NOPE:].reshape(KV_LORA, B_HEADS * V_HEAD).astype(BF16)
    woa, wob, wout = w_o_a[0].astype(BF16), w_o_b[0].astype(BF16), w_out[0].astype(BF16)
    wq_t = w_query[0].T.astype(BF16)
    sk1, sk2 = sub_keys1[0].astype(BF16), sub_keys2[0].astype(BF16)
    down, up_t = expert_down[0].astype(BF16), expert_up[0].astype(BF16).T
    bias_w, bias_m = _window_bias(rel_table, attn_sink[0])

    zm_qkv = norm_matmul(meta_tokens, norm_mix[0], w_qkv, BF16)
    zm_lat = norm_matmul(meta_tokens, norm_mix[0], w_lat, F32)
    _, km, vm = mla_proj(zm_lat, g_cq[0], g_ckv[0], wq, wk, wv, _rope_tables(N_META), N_META)
    zm_qkv, km, vm = _pad_rows(zm_qkv, META_PAD), _pad_rows(km, META_PAD), _pad_rows(vm, META_PAD)

    def encode(x):
        bsz, seq, _ = x.shape
        xt = x.reshape(bsz * seq, D_MODEL)
        tabs = tuple(t[N_META:] for t in _rope_tables(N_META + seq))
        z_qkv = norm_matmul(xt, norm_mix[0], w_qkv, BF16)
        z_lat = norm_matmul(xt, norm_mix[0], w_lat, F32)
        gates = norm_matmul(xt, norm_mix[0], w_gate, BF16)
        q, k, v = mla_proj(z_lat, g_cq[0], g_ckv[0], wq, wk, wv, tabs, seq)
        y_a = window_attention(z_qkv, zm_qkv, bias_w, bias_m, bsz, seq)
        y_b = mla_attention(q, k, v, km, vm, bsz, seq)
        h1 = merge_out_proj(y_a, y_b, woa, wob, gates, wout, xt)
        hf, th1, r1, s2, e2 = peer_route(h1, norm_ffn[0], wq_t, sk1, sk2)
        y = peer_experts(hf, down, up_t, th1, r1, s2, e2, h1, g_final)
        return y.reshape(bsz, seq, D_MODEL)

    return encode(x_prompt), encode(x_sample)
```
